```python
import math
import jax, jax.numpy as jnp
from jax import lax
import numpy as np

D_MODEL = 1024
BATCH = 8
SEQ = 2048
DEPTH = 2

N_META = 16
CHUNK = 64
PAD_FRONT = CHUNK - N_META
N_BRANCH = 3
BRANCH_WIDTH = D_MODEL
NORM_EPS = 1e-6
DT_MIN = 0.001
DT_MAX = 0.1
SSD_HEAD_DIM = 64
SSD_HEADS = BRANCH_WIDTH // SSD_HEAD_DIM
SSD_WIDTH = SSD_HEADS * SSD_HEAD_DIM
SSD_GROUPS = 2
SSD_STATE = 128
SSD_CONV = 4
SSD_XBC = SSD_WIDTH + 2 * SSD_GROUPS * SSD_STATE
SC_WIDTH = BRANCH_WIDTH
SC_CONV = 3
DN_HEAD_DIM = 128
DN_HEADS = BRANCH_WIDTH // DN_HEAD_DIM
DN_WIDTH = DN_HEADS * DN_HEAD_DIM
DN_CONV = 4
IN_SIZES = (
    SSD_WIDTH,
    SSD_XBC,
    SSD_HEADS,
    SC_WIDTH,
    SC_WIDTH,
    SC_WIDTH,
    SC_WIDTH,
    3 * DN_WIDTH,
    DN_WIDTH,
    DN_HEADS,
    DN_HEADS,
    N_BRANCH * D_MODEL,
)
IN_WIDTH = sum(IN_SIZES)

kernel_name = "hybrid_ssd_shortconv_gdn_block"


def rms_norm(x, gain):
    xf = x.astype(jnp.float32)
    y = xf * lax.rsqrt(jnp.mean(xf * xf, axis=-1, keepdims=True) + NORM_EPS)
    return (y * gain.astype(jnp.float32)).astype(x.dtype)


def l2_normalize(x):
    xf = x.astype(jnp.float32)
    return (xf * lax.rsqrt(jnp.sum(xf * xf, axis=-1, keepdims=True) + NORM_EPS)).astype(x.dtype)


def causal_depthwise_conv(x, w):
    k, ch = w.shape
    return lax.conv_general_dilated(
        x, w[:, None, :].astype(x.dtype), window_strides=(1,), padding=[(k - 1, 0)],
        dimension_numbers=("NWC", "WIO", "NWC"), feature_group_count=ch)


def pad_front(a):
    return jnp.pad(a, [(0, 0), (PAD_FRONT, 0)] + [(0, 0)] * (a.ndim - 2))


def ssd_chunked(x, dt, a, b_mat, c_mat):
    f32 = jnp.float32
    bsz, t, g, r, p = x.shape
    n = b_mat.shape[-1]
    nc = t // CHUNK
    dt = dt.astype(f32)
    xc = (x.astype(f32) * dt[..., None]).reshape(bsz, nc, CHUNK, g, r, p)
    bc = b_mat.astype(f32).reshape(bsz, nc, CHUNK, g, n)
    cc = c_mat.astype(f32).reshape(bsz, nc, CHUNK, g, n)
    a_cum = jnp.cumsum((dt * a).reshape(bsz, nc, CHUNK, g, r), axis=2)
    causal = jnp.tril(jnp.ones((CHUNK, CHUNK), bool))[:, :, None, None]
    seg = a_cum[:, :, :, None] - a_cum[:, :, None, :]
    decay = jnp.exp(jnp.where(causal, seg, -jnp.inf))
    cb = jnp.einsum("bclgn,bcsgn->bclsg", cc, bc)
    y_diag = jnp.einsum("bclsgr,bcsgrp->bclgrp", cb[..., None] * decay, xc)
    a_last = a_cum[:, :, -1]
    states = jnp.einsum("bcsgn,bcsgrp->bcgrpn", bc,
                        xc * jnp.exp(a_last[:, :, None] - a_cum)[..., None])

    def step(h, inp):
        st, dec = inp
        return h * dec[..., None, None] + st, h

    h0 = jnp.zeros((bsz, g, r, p, n), f32)
    _, h_prev = lax.scan(step, h0, (jnp.moveaxis(states, 1, 0), jnp.moveaxis(jnp.exp(a_last), 1, 0)))
    h_prev = jnp.moveaxis(h_prev, 0, 1)
    y_off = jnp.einsum("bclgn,bcgrpn->bclgrp", cc, h_prev) * jnp.exp(a_cum)[..., None]
    return (y_diag + y_off).reshape(bsz, t, g, r, p)


def gated_delta_chunked(q, k, v, g, beta):
    f32 = jnp.float32
    bsz, t, h, dk = q.shape
    dv = v.shape[-1]
    nc = t // CHUNK

    def chunks(a):
        a = a.astype(f32).reshape((bsz, nc, CHUNK, h) + a.shape[3:])
        return jnp.swapaxes(a, 2, 3)

    qc, kc, vc, gc, bc = chunks(q), chunks(k), chunks(v), chunks(g), chunks(beta)
    g_cum = jnp.cumsum(gc, axis=-1)
    causal = jnp.tril(jnp.ones((CHUNK, CHUNK), bool))
    strict = jnp.tril(jnp.ones((CHUNK, CHUNK), bool), k=-1)
    decay = jnp.exp(jnp.where(causal, g_cum[..., :, None] - g_cum[..., None, :], -jnp.inf))
    kk = jnp.einsum("bchld,bchsd->bchls", kc, kc)
    lmat = jnp.where(strict, bc[..., None] * kk * decay, 0.0)
    rhs = jnp.concatenate([vc * bc[..., None], kc * (bc * jnp.exp(g_cum))[..., None]], axis=-1)
    sol = lax.linalg.triangular_solve(jnp.eye(CHUNK, dtype=f32) + lmat, rhs,
                                      left_side=True, lower=True)
    u, w = sol[..., :dv], sol[..., dv:]
    g_last = g_cum[..., -1]
    k_dec = kc * jnp.exp(g_last[..., None] - g_cum)[..., None]

    def step(s, inp):
        u_c, w_c, kd_c, gl_c = inp
        v_new = u_c - jnp.einsum("bhlk,bhkv->bhlv", w_c, s)
        s_next = s * jnp.exp(gl_c)[..., None, None] + jnp.einsum("bhlk,bhlv->bhkv", kd_c, v_new)
        return s_next, (s, v_new)

    s0 = jnp.zeros((bsz, h, dk, dv), f32)
    xs = (jnp.moveaxis(u, 1, 0), jnp.moveaxis(w, 1, 0), jnp.moveaxis(k_dec, 1, 0), jnp.moveaxis(g_last, 1, 0))
    _, (s_prev, v_new) = lax.scan(step, s0, xs)
    s_prev = jnp.moveaxis(s_prev, 0, 1)
    v_new = jnp.moveaxis(v_new, 0, 1)
    qk = jnp.einsum("bchld,bchsd->bchls", qc, kc) * decay
    o = (jnp.einsum("bchlk,bchkv->bchlv", qc * jnp.exp(g_cum)[..., None], s_prev)
         + jnp.einsum("bchls,bchsv->bchlv", qk, v_new))
    return jnp.swapaxes(o, 2, 3).reshape(bsz, t, h, dv)


def ssd_mixer(z, xbc, dt_raw, conv_w, conv_b, dt_bias, a_log, d_skip, norm_w):
    bsz, t, _ = z.shape
    r = SSD_HEADS // SSD_GROUPS
    xbc = jax.nn.silu(causal_depthwise_conv(xbc, conv_w) + conv_b.astype(xbc.dtype))
    xs, b_mat, c_mat = jnp.split(xbc, [SSD_WIDTH, SSD_WIDTH + SSD_GROUPS * SSD_STATE], axis=-1)
    xh = xs.reshape(bsz, t, SSD_GROUPS, r, SSD_HEAD_DIM)
    dt = jax.nn.softplus(dt_raw.astype(jnp.float32) + dt_bias.astype(jnp.float32))
    a = -jnp.exp(a_log.astype(jnp.float32)).reshape(SSD_GROUPS, r)
    y = ssd_chunked(pad_front(xh), pad_front(dt.reshape(bsz, t, SSD_GROUPS, r)), a,
                    pad_front(b_mat.reshape(bsz, t, SSD_GROUPS, SSD_STATE)),
                    pad_front(c_mat.reshape(bsz, t, SSD_GROUPS, SSD_STATE)))[:, PAD_FRONT:]
    y = y + d_skip.astype(jnp.float32).reshape(SSD_GROUPS, r)[..., None] * xh.astype(jnp.float32)
    y = y.astype(z.dtype).reshape(bsz, t, SSD_WIDTH) * jax.nn.silu(z)
    gs = SSD_WIDTH // SSD_GROUPS
    y = rms_norm(y.reshape(bsz, t, SSD_GROUPS, gs), norm_w.reshape(SSD_GROUPS, gs))
    return y.reshape(bsz, t, SSD_WIDTH)


def short_conv_mixer(b_gate, c_gate, h, gate, conv_w):
    y = b_gate * causal_depthwise_conv(c_gate * h, conv_w)
    return y * jax.nn.silu(gate)


def gated_deltanet_mixer(qkv, z, b_raw, a_raw, conv_w, dt_bias, a_log, norm_w):
    bsz, t, _ = qkv.shape
    qkv = jax.nn.silu(causal_depthwise_conv(qkv, conv_w))
    q, k, v = jnp.split(qkv, 3, axis=-1)
    q = l2_normalize(q.reshape(bsz, t, DN_HEADS, DN_HEAD_DIM)) * (DN_HEAD_DIM ** -0.5)
    k = l2_normalize(k.reshape(bsz, t, DN_HEADS, DN_HEAD_DIM))
    v = v.reshape(bsz, t, DN_HEADS, DN_HEAD_DIM)
    beta = jax.nn.sigmoid(b_raw.astype(jnp.float32))
    g = -jnp.exp(a_log.astype(jnp.float32)) * jax.nn.softplus(a_raw.astype(jnp.float32) + dt_bias.astype(jnp.float32))
    o = gated_delta_chunked(pad_front(q), pad_front(k), pad_front(v), pad_front(g),
                            pad_front(beta))[:, PAD_FRONT:]
    o = rms_norm(o.astype(z.dtype), norm_w) * jax.nn.silu(z.reshape(bsz, t, DN_HEADS, DN_HEAD_DIM))
    return o.reshape(bsz, t, DN_WIDTH)


def hybrid_layer(x, norm_pre, norm_post, w_in, ssd_conv_w, ssd_conv_b, ssd_dt_bias, ssd_a_log,
                 ssd_d, ssd_norm, sc_conv_w, dn_conv_w, dn_dt_bias, dn_a_log, dn_norm,
                 w_branch, w_out):
    bsz, t, _ = x.shape
    xn = rms_norm(x, norm_pre)
    proj = xn @ w_in.astype(x.dtype)
    (ssd_z, ssd_xbc, ssd_dt, sc_b, sc_c, sc_h, sc_g,
     dn_qkv, dn_z, dn_b, dn_a, gate_logits) = jnp.split(
        proj, np.cumsum(IN_SIZES)[:-1].tolist(), axis=-1)
    y_ssd = ssd_mixer(ssd_z, ssd_xbc, ssd_dt, ssd_conv_w, ssd_conv_b, ssd_dt_bias, ssd_a_log, ssd_d, ssd_norm)
    y_sc = short_conv_mixer(sc_b, sc_c, sc_h, sc_g, sc_conv_w)
    y_dn = gated_deltanet_mixer(dn_qkv, dn_z, dn_b, dn_a, dn_conv_w, dn_dt_bias, dn_a_log, dn_norm)
    ys = jnp.stack([y_ssd, y_sc, y_dn], axis=2)
    branch = jnp.einsum("btnw,nwd->btnd", ys, w_branch.astype(x.dtype))
    gates = jax.nn.sigmoid(gate_logits.reshape(bsz, t, N_BRANCH, D_MODEL))
    merged = jnp.sum(gates * branch, axis=2)
    out = merged @ w_out.astype(x.dtype)
    return x + rms_norm(out, norm_post)


def setup_inputs(seed: int = 0) -> dict:
    key = jax.random.key(seed)
    ks = jax.random.split(key, 18)
    f32 = jnp.float32

    def normal(k, shape, scale):
        return jax.random.normal(k, shape, f32) * scale

    def gain(k, shape):
        return 1.0 + 0.02 * jax.random.normal(k, shape, f32)

    def dt_bias(k, shape):
        u = jax.random.uniform(k, shape, f32)
        dt = jnp.exp(u * (math.log(DT_MAX) - math.log(DT_MIN)) + math.log(DT_MIN))
        return dt + jnp.log(-jnp.expm1(-dt))

    def a_log(k, shape):
        return jnp.log(jax.random.uniform(k, shape, f32, 1.0, 16.0))

    return {
        "x": normal(ks[0], (BATCH, SEQ, D_MODEL), 1.0),
        "meta_tokens": normal(ks[1], (N_META, D_MODEL), 1.0),
        "norm_pre": gain(ks[2], (DEPTH, D_MODEL)),
        "norm_post": gain(ks[3], (DEPTH, D_MODEL)),
        "w_in": normal(ks[4], (DEPTH, D_MODEL, IN_WIDTH), D_MODEL ** -0.5),
        "ssd_conv_w": normal(ks[5], (DEPTH, SSD_CONV, SSD_XBC), SSD_CONV ** -0.5),
        "ssd_conv_b": normal(ks[6], (DEPTH, SSD_XBC), 0.02),
        "ssd_dt_bias": dt_bias(ks[7], (DEPTH, SSD_HEADS)),
        "ssd_a_log": a_log(ks[8], (DEPTH, SSD_HEADS)),
        "ssd_d": gain(ks[9], (DEPTH, SSD_HEADS)),
        "ssd_norm": gain(ks[10], (DEPTH, SSD_WIDTH)),
        "sc_conv_w": normal(ks[11], (DEPTH, SC_CONV, SC_WIDTH), SC_CONV ** -0.5),
        "dn_conv_w": normal(ks[12], (DEPTH, DN_CONV, 3 * DN_WIDTH), DN_CONV ** -0.5),
        "dn_dt_bias": dt_bias(ks[13], (DEPTH, DN_HEADS)),
        "dn_a_log": a_log(ks[14], (DEPTH, DN_HEADS)),
        "dn_norm": gain(ks[15], (DEPTH, DN_HEAD_DIM)),
        "w_branch": normal(ks[16], (DEPTH, N_BRANCH, BRANCH_WIDTH, D_MODEL), BRANCH_WIDTH ** -0.5),
        "w_out": normal(ks[17], (DEPTH, D_MODEL, D_MODEL), D_MODEL ** -0.5),
    }


def reference(x, meta_tokens, norm_pre, norm_post, w_in, ssd_conv_w, ssd_conv_b, ssd_dt_bias,
              ssd_a_log, ssd_d, ssd_norm, sc_conv_w, dn_conv_w, dn_dt_bias, dn_a_log, dn_norm,
              w_branch, w_out):
    bsz = x.shape[0]
    meta = jnp.broadcast_to(meta_tokens.astype(x.dtype)[None], (bsz, N_META, D_MODEL))
    h = jnp.concatenate([meta, x], axis=1)
    for i in range(DEPTH):
        h = hybrid_layer(h, norm_pre[i], norm_post[i], w_in[i], ssd_conv_w[i], ssd_conv_b[i],
                         ssd_dt_bias[i], ssd_a_log[i], ssd_d[i], ssd_norm[i], sc_conv_w[i],
                         dn_conv_w[i], dn_dt_bias[i], dn_a_log[i], dn_norm[i], w_branch[i], w_out[i])
    return h[:, N_META:]
```

```python
import functools

import jax
import jax.numpy as jnp
from jax import lax
from jax.experimental import pallas as pl
from jax.experimental.pallas import tpu as pltpu

F32 = jnp.float32
BF16 = jnp.bfloat16

D_MODEL = 1024
N_META = 16
CHUNK = 64
PAD_FRONT = CHUNK - N_META
NORM_EPS = 1e-6
SSD_HEADS = 16
SSD_HEAD_DIM = 64
SSD_GROUPS = 2
SSD_STATE = 128
SSD_XBC = 1536
DN_HEADS = 8
DN_HEAD_DIM = 128
LANES = 128
HALO = 16
PROJ_TN = 1536
VMEM_LIMIT = 52 * 1024 * 1024

MAIN_WIDTH = 13824
COL_QKV = 0
COL_GATES = 1
COL_SSD_Z = 6
COL_SC_B = 7
COL_SC_C = 8
COL_SC_H = 9
COL_SC_G = 10
COL_DN_Z = 11
COL_XBC = 8
LANE_DT = 0
LANE_BETA = 16
LANE_A = 24


def _dot(a, b):
    return jnp.dot(a, b, preferred_element_type=F32)


def _dot_nt(a, b):
    return lax.dot_general(a, b, (((1,), (1,)), ((), ())), preferred_element_type=F32)


def _dot_tn(a, b):
    return lax.dot_general(a, b, (((0,), (0,)), ((), ())), preferred_element_type=F32)


def _split3(x):
    hi = x.astype(BF16)
    r1 = x - hi.astype(F32)
    mid = r1.astype(BF16)
    lo = (r1 - mid.astype(F32)).astype(BF16)
    return hi, mid, lo


def _dot_sel(x, sel):
    hi, mid, lo = _split3(x)
    return _dot(hi, sel) + _dot(mid, sel) + _dot(lo, sel)


def _sel_dot(sel, x):
    hi, mid, lo = _split3(x)
    return _dot(sel, hi) + _dot(sel, mid) + _dot(sel, lo)


def _sel_dot_nt(sel, x):
    hi, mid, lo = _split3(x)
    return _dot_nt(sel, hi) + _dot_nt(sel, mid) + _dot_nt(sel, lo)


def _sigmoid(x):
    return 1.0 / (1.0 + jnp.exp(-x))


def _silu(x):
    return x * _sigmoid(x)


def _softplus(x):
    return jnp.maximum(x, 0.0) + jnp.log(1.0 + jnp.exp(-jnp.abs(x)))


def _iota(shape, dim):
    return lax.broadcasted_iota(jnp.int32, shape, dim)


def _tril_bf16(n=CHUNK):
    return (_iota((n, n), 0) >= _iota((n, n), 1)).astype(BF16)


def _eye_bf16(n=LANES):
    return (_iota((n, n), 0) == _iota((n, n), 1)).astype(BF16)


def _causal_conv(x_ref, halo_ref, xpad_ref, w_ref, cs, first, taps):
    rows = x_ref.shape[0]
    halo = halo_ref[:, cs].astype(F32)
    xpad_ref[0:HALO, :] = jnp.where(first, 0.0, halo)
    xpad_ref[HALO:HALO + rows, :] = x_ref[:, cs].astype(F32)
    acc = None
    for k in range(taps):
        term = w_ref[k:k + 1, cs] * xpad_ref[pl.ds(HALO - taps + 1 + k, rows), :]
        acc = term if acc is None else acc + term
    return acc


def _in_proj_kernel(x_ref, gain_ref, w_ref, ws_ref, main_ref, small_ref, xn_ref):
    @pl.when(pl.program_id(1) == 0)
    def _():
        x = x_ref[...]
        ms = jnp.mean(x * x, axis=-1, keepdims=True)
        xn = (x * lax.rsqrt(ms + NORM_EPS) * gain_ref[...]).astype(BF16)
        xn_ref[...] = xn
        small_ref[...] = _dot(xn, ws_ref[...])

    main_ref[...] = _dot(xn_ref[...], w_ref[...]).astype(BF16)


def _in_proj(h, gain, w_main, w_small, tm):
    rows = h.shape[0]
    grid = (rows // tm, MAIN_WIDTH // PROJ_TN)
    return pl.pallas_call(
        _in_proj_kernel,
        grid=grid,
        in_specs=[
            pl.BlockSpec((tm, D_MODEL), lambda i, j: (i, 0)),
            pl.BlockSpec((1, D_MODEL), lambda i, j: (0, 0)),
            pl.BlockSpec((D_MODEL, PROJ_TN), lambda i, j: (0, j)),
            pl.BlockSpec((D_MODEL, LANES), lambda i, j: (0, 0)),
        ],
        out_specs=[
            pl.BlockSpec((tm, PROJ_TN), lambda i, j: (i, j)),
            pl.BlockSpec((tm, LANES), lambda i, j: (i, 0)),
        ],
        out_shape=[
            jax.ShapeDtypeStruct((rows, MAIN_WIDTH), BF16),
            jax.ShapeDtypeStruct((rows, LANES), F32),
        ],
        scratch_shapes=[pltpu.VMEM((tm, D_MODEL), BF16)],
        compiler_params=pltpu.CompilerParams(
            dimension_semantics=("arbitrary", "arbitrary"), vmem_limit_bytes=VMEM_LIMIT),
        name="in_proj",
    )(h, gain, w_main, w_small)


def _dn_prep_kernel(qkv_ref, halo_ref, small_ref, cw_ref, vec_ref, eg_ref, eb_ref,
                    qkvc_ref, gb_ref, l_ref, a_ref, xpad_ref, ge_ref, be_ref, *, chunks):
    t = pl.program_id(1)
    rows = chunks * CHUNK
    first = t == 0
    width = 512
    for j in range(3 * D_MODEL // width):
        cs = slice(j * width, (j + 1) * width)
        val = _silu(_causal_conv(qkv_ref, halo_ref, xpad_ref, cw_ref, cs, first, 4))
        if j < 2 * D_MODEL // width:
            scale = DN_HEAD_DIM ** -0.5 if j < D_MODEL // width else 1.0
            for i in range(width // DN_HEAD_DIM):
                hs = slice(i * DN_HEAD_DIM, (i + 1) * DN_HEAD_DIM)
                vh = val[:, hs]
                ss = jnp.sum(vh * vh, axis=-1, keepdims=True)
                vh = vh * (lax.rsqrt(ss + NORM_EPS) * scale)
                qkvc_ref[:, j * width + i * DN_HEAD_DIM:j * width + (i + 1) * DN_HEAD_DIM] = vh.astype(BF16)
        else:
            qkvc_ref[:, cs] = val.astype(BF16)

    sm = small_ref[...]
    lane = _iota(sm.shape, 1)
    real = (t * rows + _iota(sm.shape, 0)) >= PAD_FRONT
    is_beta = (lane >= LANE_BETA) & (lane < LANE_BETA + DN_HEADS)
    is_a = (lane >= LANE_A) & (lane < LANE_A + DN_HEADS)
    beta = _sigmoid(sm)
    g = -jnp.exp(vec_ref[1:2, :]) * _softplus(sm + vec_ref[0:1, :])
    gb_ref[...] = jnp.where(real & is_beta, beta, jnp.where(real & is_a, g, 0.0))

    tril = _tril_bf16()
    eye = _eye_bf16()

    def cum_body(c, carry):
        r = pl.ds(pl.multiple_of(c * CHUNK, CHUNK), CHUNK)
        blk = gb_ref[r, :]
        lane_c = _iota(blk.shape, 1)
        is_a_c = (lane_c >= LANE_A) & (lane_c < LANE_A + DN_HEADS)
        cum = _sel_dot(tril, jnp.where(is_a_c, blk, 0.0))
        gb_ref[r, :] = jnp.where(is_a_c, cum, blk)
        return carry

    lax.fori_loop(0, chunks, cum_body, 0)

    gb = gb_ref[...]
    ge_ref[...] = _dot_sel(gb, eg_ref[...])
    be_ref[...] = _dot_sel(gb, eb_ref[...])

    li = _iota((CHUNK, CHUNK), 0)
    si = _iota((CHUNK, CHUNK), 1)
    causal = li >= si
    strict = li > si

    def chunk_body(c, carry):
        r = pl.ds(pl.multiple_of(c * CHUNK, CHUNK), CHUNK)
        gct = _sel_dot_nt(eye, gb_ref[r, :])
        for h in range(DN_HEADS):
            hs = slice(h * DN_HEAD_DIM, h * DN_HEAD_DIM + CHUNK)
            q = qkvc_ref[r, h * DN_HEAD_DIM:(h + 1) * DN_HEAD_DIM]
            k = qkvc_ref[r, D_MODEL + h * DN_HEAD_DIM:D_MODEL + (h + 1) * DN_HEAD_DIM]
            gl = ge_ref[r, hs]
            gs = gct[LANE_A + h:LANE_A + h + 1, :]
            decay = jnp.where(causal, jnp.exp(jnp.minimum(gl - gs, 0.0)), 0.0)
            kk = _dot_nt(k, k)
            qk = _dot_nt(q, k)
            l_ref[c, h] = jnp.where(strict, be_ref[r, hs] * kk * decay, 0.0)
            a_ref[c, h] = (qk * decay).astype(BF16)
        return carry

    lax.fori_loop(0, chunks, chunk_body, 0)


def _row_specs(rt, nt):
    def tile(width, col):
        return pl.BlockSpec((rt, width), lambda b, t: (b * nt + t, col))

    def halo(width, col):
        return pl.BlockSpec(
            (HALO, width), lambda b, t: (jnp.maximum((b * nt + t) * (rt // HALO) - 1, 0), col))

    def const(shape):
        return pl.BlockSpec(shape, lambda b, t: (0,) * len(shape), pipeline_mode=pl.Buffered(1))

    return tile, halo, const


def _dn_prep(main, small, conv_w, vec, eg, eb, batch, nc, chunks):
    rows = main.shape[0]
    nt = nc // chunks
    rt = chunks * CHUNK
    tile, halo, const = _row_specs(rt, nt)
    inst_spec = pl.BlockSpec((chunks, DN_HEADS, CHUNK, CHUNK), lambda b, t: (b * nt + t, 0, 0, 0))
    return pl.pallas_call(
        functools.partial(_dn_prep_kernel, chunks=chunks),
        grid=(batch, nt),
        in_specs=[
            tile(3 * D_MODEL, COL_QKV), halo(3 * D_MODEL, COL_QKV), tile(LANES, 0),
            const((4, 3 * D_MODEL)), const((8, LANES)),
            const((LANES, D_MODEL)), const((LANES, D_MODEL)),
        ],
        out_specs=[tile(3 * D_MODEL, 0), tile(LANES, 0), inst_spec, inst_spec],
        out_shape=[
            jax.ShapeDtypeStruct((rows, 3 * D_MODEL), BF16),
            jax.ShapeDtypeStruct((rows, LANES), F32),
            jax.ShapeDtypeStruct((batch * nc, DN_HEADS, CHUNK, CHUNK), F32),
            jax.ShapeDtypeStruct((batch * nc, DN_HEADS, CHUNK, CHUNK), BF16),
        ],
        scratch_shapes=[
            pltpu.VMEM((HALO + rt, 512), F32),
            pltpu.VMEM((rt, D_MODEL), F32),
            pltpu.VMEM((rt, D_MODEL), F32),
        ],
        compiler_params=pltpu.CompilerParams(
            dimension_semantics=("arbitrary", "arbitrary"), vmem_limit_bytes=VMEM_LIMIT),
        name="dn_prep",
    )(main, main, small, conv_w, vec, eg, eb)


def _dn_solve_kernel(l_ref, t_ref):
    col = _iota((CHUNK, LANES), 0)
    sub = 8

    def init_body(i, carry):
        t_ref[i] = (col == i).astype(F32)
        return carry

    lax.fori_loop(0, CHUNK, init_body, 0)

    def row_body(i, carry):
        def inner(jb, acc):
            j0 = pl.multiple_of(jb * sub, sub)
            blk = l_ref[i, pl.ds(j0, sub), :]
            for jj in range(sub):
                acc = acc - blk[jj:jj + 1, :] * t_ref[j0 + jj]
            return acc

        nblk = lax.shift_right_logical(i + (sub - 1), 3)
        t_ref[i] = lax.fori_loop(0, nblk, inner, (col == i).astype(F32))
        return carry

    lax.fori_loop(1, CHUNK, row_body, 0)


def _dn_solve(lt):
    ni = lt.shape[-1]
    spec = pl.BlockSpec((CHUNK, CHUNK, LANES), lambda i: (0, 0, i))
    return pl.pallas_call(
        _dn_solve_kernel,
        grid=(pl.cdiv(ni, LANES),),
        in_specs=[spec],
        out_specs=spec,
        out_shape=jax.ShapeDtypeStruct(lt.shape, F32),
        compiler_params=pltpu.CompilerParams(dimension_semantics=("arbitrary",)),
        name="dn_solve",
    )(lt)


def _dn_scan_kernel(qkvc_ref, gb_ref, t_ref, a_ref, z_ref, eg_ref, eb_ref, nw_ref,
                    y_ref, s_ref, ge_ref, be_ref, *, chunks):
    @pl.when(pl.program_id(1) == 0)
    def _():
        s_ref[...] = jnp.zeros_like(s_ref)

    gb = gb_ref[...]
    ge_ref[...] = _dot_sel(gb, eg_ref[...])
    be_ref[...] = _dot_sel(gb, eb_ref[...])
    eye = _eye_bf16()
    nw = nw_ref[...]

    def chunk_body(c, carry):
        base = pl.multiple_of(c * CHUNK, CHUNK)
        r = pl.ds(base, CHUNK)
        for h in range(DN_HEADS):
            hs = slice(h * DN_HEAD_DIM, (h + 1) * DN_HEAD_DIM)
            gc = ge_ref[r, hs]
            gl = ge_ref[pl.ds(base + CHUNK - 8, 8), hs][7:8, :]
            bt = be_ref[r, hs]
            eg = jnp.exp(gc)
            q = qkvc_ref[r, hs].astype(F32)
            k = qkvc_ref[r, D_MODEL + h * DN_HEAD_DIM:D_MODEL + (h + 1) * DN_HEAD_DIM].astype(F32)
            v = qkvc_ref[r, 2 * D_MODEL + h * DN_HEAD_DIM:2 * D_MODEL + (h + 1) * DN_HEAD_DIM].astype(F32)
            rhs = jnp.concatenate([v * bt, k * (bt * eg)], axis=1).astype(BF16)
            uw = _dot(t_ref[c, h].astype(BF16), rhs)
            u = uw[:, :DN_HEAD_DIM]
            w = uw[:, DN_HEAD_DIM:]
            s = s_ref[h]
            sb = s.astype(BF16)
            vn = u - _dot(w.astype(BF16), sb)
            vnb = vn.astype(BF16)
            o = _dot((q * eg).astype(BF16), sb) + _dot(a_ref[c, h], vnb)
            kd = (k * jnp.exp(gl - gc)).astype(BF16)
            kdt = _dot_nt(eye, kd).astype(BF16)
            s_ref[h] = s * jnp.exp(gl) + _dot(kdt, vnb)
            ms = jnp.mean(o * o, axis=-1, keepdims=True)
            on = o * lax.rsqrt(ms + NORM_EPS) * nw
            y_ref[r, hs] = (on * _silu(z_ref[r, hs].astype(F32))).astype(BF16)
        return carry

    lax.fori_loop(0, chunks, chunk_body, 0)


def _dn_scan(qkvc, gb, tmat, amat, main, eg, eb, norm_w, batch, nc, chunks):
    rows = qkvc.shape[0]
    nt = nc // chunks
    rt = chunks * CHUNK
    tile, _, const = _row_specs(rt, nt)
    inst_spec = pl.BlockSpec((chunks, DN_HEADS, CHUNK, CHUNK), lambda b, t: (b * nt + t, 0, 0, 0))
    return pl.pallas_call(
        functools.partial(_dn_scan_kernel, chunks=chunks),
        grid=(batch, nt),
        in_specs=[
            tile(3 * D_MODEL, 0), tile(LANES, 0), inst_spec, inst_spec, tile(D_MODEL, COL_DN_Z),
            const((LANES, D_MODEL)), const((LANES, D_MODEL)), const((1, DN_HEAD_DIM)),
        ],
        out_specs=tile(D_MODEL, 0),
        out_shape=jax.ShapeDtypeStruct((rows, D_MODEL), BF16),
        scratch_shapes=[
            pltpu.VMEM((DN_HEADS, DN_HEAD_DIM, DN_HEAD_DIM), F32),
            pltpu.VMEM((rt, D_MODEL), F32),
            pltpu.VMEM((rt, D_MODEL), F32),
        ],
        compiler_params=pltpu.CompilerParams(
            dimension_semantics=("arbitrary", "arbitrary"), vmem_limit_bytes=VMEM_LIMIT),
        name="dn_scan",
    )(qkvc, gb, tmat, amat, main, eg, eb, norm_w)


def _ssd_kernel(xbc_ref, halo_ref, z_ref, small_ref, cw_ref, cb_ref, vec_ref, e16_ref, psel_ref,
                de_ref, nw_ref, y_ref, xpad_ref, xs_ref, bc_ref, acum_ref, ae_ref, xc_ref, h_ref,
                *, chunks):
    t = pl.program_id(1)
    rows = chunks * CHUNK
    first = t == 0

    @pl.when(first)
    def _():
        h_ref[...] = jnp.zeros_like(h_ref)

    width = 512
    for j in range(SSD_XBC // width):
        cs = slice(j * width, (j + 1) * width)
        val = _silu(_causal_conv(xbc_ref, halo_ref, xpad_ref, cw_ref, cs, first, 4) + cb_ref[:, cs])
        if j < 2:
            xs_ref[:, cs] = val
        else:
            bc_ref[...] = val

    sm = small_ref[...]
    lane = _iota(sm.shape, 1)
    real = (t * rows + _iota(sm.shape, 0)) >= PAD_FRONT
    dt = jnp.where(real & (lane < SSD_HEADS), _softplus(sm + vec_ref[0:1, :]), 0.0)
    acum_ref[...] = dt * (-jnp.exp(vec_ref[1:2, :]))
    xc_ref[...] = xs_ref[...] * _dot_sel(dt, e16_ref[...])

    tril = _tril_bf16()

    def cum_body(c, carry):
        r = pl.ds(pl.multiple_of(c * CHUNK, CHUNK), CHUNK)
        acum_ref[r, :] = _sel_dot(tril, acum_ref[r, :])
        return carry

    lax.fori_loop(0, chunks, cum_body, 0)
    ae_ref[...] = _dot_sel(acum_ref[...], e16_ref[...])

    eye = _eye_bf16()
    psel = psel_ref[...]
    prow = _iota((2 * CHUNK, LANES), 0)
    plane = _iota((2 * CHUNK, LANES), 1)
    keep = (prow < CHUNK) == ((plane & 1) == 0)
    bd_mask = (prow < CHUNK) == (plane < CHUNK)
    causal2 = _iota((CHUNK, LANES), 0) >= (_iota((CHUNK, LANES), 1) & (CHUNK - 1))
    half = SSD_HEADS // SSD_GROUPS * SSD_HEAD_DIM
    nw = nw_ref[...]
    de = de_ref[...]

    def chunk_body(c, carry):
        base = pl.multiple_of(c * CHUNK, CHUNK)
        r = pl.ds(base, CHUNK)
        ae = ae_ref[r, :]
        alast = ae_ref[pl.ds(base + CHUNK - 8, 8), :][7:8, :]
        xc = xc_ref[r, :]
        xdec = (xc * jnp.exp(alast - ae)).astype(BF16)
        ac = acum_ref[r, :]
        a2 = jnp.where(keep, jnp.concatenate([ac, ac], axis=0), 0.0)
        rowpair = _sel_dot_nt(psel, a2)
        bc = bc_ref[r, :]
        ys = []
        for g in range(SSD_GROUPS):
            gs = slice(g * half, (g + 1) * half)
            bg = bc[:, g * SSD_STATE:(g + 1) * SSD_STATE].astype(BF16)
            cg = bc[:, (SSD_GROUPS + g) * SSD_STATE:(SSD_GROUPS + g + 1) * SSD_STATE].astype(BF16)
            hg = h_ref[g]
            yoff = _dot(cg, hg.astype(BF16)) * jnp.exp(ae[:, gs])
            bgt = _dot_nt(eye, bg).astype(BF16)
            h_ref[g] = hg * jnp.exp(alast[:, gs]) + _dot(bgt, xdec[:, gs])
            cbcb = _dot_nt(cg, jnp.concatenate([bg, bg], axis=0))
            for p in range(SSD_HEADS // SSD_GROUPS // 2):
                j = g * (SSD_HEADS // SSD_GROUPS // 2) + p
                js = slice(j * LANES, (j + 1) * LANES)
                diff = ae[:, js] - rowpair[j:j + 1, :]
                m = jnp.where(causal2, cbcb * jnp.exp(jnp.minimum(diff, 0.0)), 0.0).astype(BF16)
                xp = xc[:, js].astype(BF16)
                bd = jnp.where(bd_mask, jnp.concatenate([xp, xp], axis=0), jnp.zeros((), BF16))
                ys.append(_dot(m, bd) + yoff[:, p * LANES:(p + 1) * LANES])
        y = jnp.concatenate(ys, axis=1) + de * xs_ref[r, :]
        y = y * _silu(z_ref[r, :].astype(F32))
        for g in range(SSD_GROUPS):
            gs = slice(g * half, (g + 1) * half)
            yg = y[:, gs]
            ms = jnp.mean(yg * yg, axis=-1, keepdims=True)
            y_ref[r, gs] = (yg * lax.rsqrt(ms + NORM_EPS) * nw[:, gs]).astype(BF16)
        return carry

    lax.fori_loop(0, chunks, chunk_body, 0)


def _ssd(main, small, conv_w, conv_b, vec, e16, psel, d_exp, norm_w, batch, nc, chunks):
    rows = main.shape[0]
    nt = nc // chunks
    rt = chunks * CHUNK
    tile, halo, const = _row_specs(rt, nt)
    return pl.pallas_call(
        functools.partial(_ssd_kernel, chunks=chunks),
        grid=(batch, nt),
        in_specs=[
            tile(SSD_XBC, COL_XBC), halo(SSD_XBC, COL_XBC), tile(D_MODEL, COL_SSD_Z), tile(LANES, 0),
            const((4, SSD_XBC)), const((1, SSD_XBC)), const((8, LANES)),
            const((LANES, D_MODEL)), const((8, LANES)), const((1, D_MODEL)), const((1, D_MODEL)),
        ],
        out_specs=tile(D_MODEL, 0),
        out_shape=jax.ShapeDtypeStruct((rows, D_MODEL), BF16),
        scratch_shapes=[
            pltpu.VMEM((HALO + rt, 512), F32),
            pltpu.VMEM((rt, D_MODEL), F32),
            pltpu.VMEM((rt, 512), F32),
            pltpu.VMEM((rt, LANES), F32),
            pltpu.VMEM((rt, D_MODEL), F32),
            pltpu.VMEM((rt, D_MODEL), F32),
            pltpu.VMEM((SSD_GROUPS, SSD_STATE, D_MODEL // SSD_GROUPS), F32),
        ],
        compiler_params=pltpu.CompilerParams(
            dimension_semantics=("arbitrary", "arbitrary"), vmem_limit_bytes=VMEM_LIMIT),
        name="ssd",
    )(main, main, main, small, conv_w, conv_b, vec, e16, psel, d_exp, norm_w)


def _out_kernel(yssd_ref, ydn_ref, b_ref, c_ref, chalo_ref, h_ref, hhalo_ref, g_ref, gates_ref,
                x_ref, scw_ref, wb_ref, wo_ref, np_ref, o_ref, upad_ref):
    t = pl.program_id(1)
    rows = x_ref.shape[0]
    first = t == 0
    uh = chalo_ref[...].astype(F32) * hhalo_ref[...].astype(F32)
    upad_ref[0:HALO, :] = jnp.where(first, 0.0, uh)
    upad_ref[HALO:HALO + rows, :] = c_ref[...].astype(F32) * h_ref[...].astype(F32)
    conv = None
    for k in range(3):
        term = scw_ref[k:k + 1, :] * upad_ref[pl.ds(HALO - 2 + k, rows), :]
        conv = term if conv is None else conv + term
    ysc = (b_ref[...].astype(F32) * conv * _silu(g_ref[...].astype(F32))).astype(BF16)

    merged = None
    for n, y in enumerate((yssd_ref[...], ysc, ydn_ref[...])):
        gate = _sigmoid(gates_ref[:, n * D_MODEL:(n + 1) * D_MODEL].astype(F32))
        term = gate * _dot(y, wb_ref[n])
        merged = term if merged is None else merged + term
    out = _dot(merged.astype(BF16), wo_ref[...])
    ms = jnp.mean(out * out, axis=-1, keepdims=True)
    res = x_ref[...] + out * lax.rsqrt(ms + NORM_EPS) * np_ref[...]
    real = (t * rows + _iota(res.shape, 0)) >= PAD_FRONT
    o_ref[...] = jnp.where(real, res, 0.0)


def _out(yssd, ydn, main, x, sc_w, w_branch, w_out, norm_post, batch, tp, rt):
    rows = x.shape[0]
    nt = tp // rt
    tile, halo, const = _row_specs(rt, nt)
    return pl.pallas_call(
        _out_kernel,
        grid=(batch, nt),
        in_specs=[
            tile(D_MODEL, 0), tile(D_MODEL, 0),
            tile(D_MODEL, COL_SC_B), tile(D_MODEL, COL_SC_C), halo(D_MODEL, COL_SC_C),
            tile(D_MODEL, COL_SC_H), halo(D_MODEL, COL_SC_H), tile(D_MODEL, COL_SC_G),
            tile(3 * D_MODEL, COL_GATES), tile(D_MODEL, 0),
            const((3, D_MODEL)), const((3, D_MODEL, D_MODEL)), const((D_MODEL, D_MODEL)),
            const((1, D_MODEL)),
        ],
        out_specs=tile(D_MODEL, 0),
        out_shape=jax.ShapeDtypeStruct((rows, D_MODEL), F32),
        scratch_shapes=[pltpu.VMEM((HALO + rt, D_MODEL), F32)],
        compiler_params=pltpu.CompilerParams(
            dimension_semantics=("arbitrary", "arbitrary"), vmem_limit_bytes=VMEM_LIMIT),
        name="out_proj",
    )(yssd, ydn, main, main, main, main, main, main, main, x, sc_w, w_branch, w_out, norm_post)


def _chunks_per_tile(nc):
    for g in (11, 3, 2, 1):
        if nc % g == 0:
            return g
    return 1


def _lane_vec(rows):
    out = jnp.zeros((8, LANES), F32)
    for i, (off, v) in enumerate(rows):
        out = out.at[i, off:off + v.shape[0]].set(v.astype(F32))
    return out


def _expand_matrix(lane0, heads, width):
    r = jnp.arange(LANES)[:, None]
    c = jnp.arange(heads * width)[None, :]
    return (r == lane0 + c // width).astype(BF16)


def _split_w_in(w):
    sizes = (1024, 1536, 16, 1024, 1024, 1024, 1024, 3072, 1024, 8, 8, 3072)
    offs = [0]
    for s in sizes:
        offs.append(offs[-1] + s)
    (ssd_z, xbc, ssd_dt, sc_b, sc_c, sc_h, sc_g, qkv, dn_z, dn_b, dn_a, gates) = [
        w[:, offs[i]:offs[i + 1]] for i in range(len(sizes))]
    main = jnp.concatenate([qkv, gates, ssd_z, sc_b, sc_c, sc_h, sc_g, dn_z, xbc], axis=1).astype(BF16)
    small = jnp.concatenate(
        [ssd_dt, dn_b, dn_a, jnp.zeros((w.shape[0], LANES - 32), w.dtype)], axis=1).astype(BF16)
    return main, small


def kernel(x, meta_tokens, norm_pre, norm_post, w_in, ssd_conv_w, ssd_conv_b, ssd_dt_bias, ssd_a_log,
           ssd_d, ssd_norm, sc_conv_w, dn_conv_w, dn_dt_bias, dn_a_log, dn_norm, w_branch, w_out):
    batch, seq, d = x.shape
    assert d == D_MODEL and seq % CHUNK == 0
    tp = CHUNK + seq
    nc = tp // CHUNK
    chunks = _chunks_per_tile(nc)
    rows = batch * tp
    tm = tp // 2 if tp % 32 == 0 else tp
    rt_out = tp // 4 if tp % (4 * HALO) == 0 else tp
    depth = w_in.shape[0]

    meta = jnp.broadcast_to(meta_tokens.astype(x.dtype)[None], (batch, N_META, d))
    h = jnp.concatenate([jnp.zeros((batch, PAD_FRONT, d), x.dtype), meta, x], axis=1).reshape(rows, d)

    e16 = _expand_matrix(LANE_DT, SSD_HEADS, SSD_HEAD_DIM)
    eg = _expand_matrix(LANE_A, DN_HEADS, DN_HEAD_DIM)
    eb = _expand_matrix(LANE_BETA, DN_HEADS, DN_HEAD_DIM)
    pj = jnp.arange(8)[:, None]
    pc = jnp.arange(LANES)[None, :]
    psel = ((pc // 2 == pj) & (pc < SSD_HEADS)).astype(BF16)

    for i in range(depth):
        w_main, w_small = _split_w_in(w_in[i])
        main, small = _in_proj(h, norm_pre[i][None, :], w_main, w_small, tm)

        dn_vec = _lane_vec([(LANE_A, dn_dt_bias[i]), (LANE_A, dn_a_log[i])])
        qkvc, gb, lmat, amat = _dn_prep(main, small, dn_conv_w[i], dn_vec, eg, eb, batch, nc, chunks)
        ni = batch * nc * DN_HEADS
        lt = jnp.transpose(lmat.reshape(ni, CHUNK, CHUNK), (1, 2, 0))
        lt = jnp.pad(lt, ((0, 0), (0, 0), (0, -ni % LANES)))
        tmat = jnp.transpose(_dn_solve(lt)[:, :, :ni], (2, 0, 1)).reshape(
            batch * nc, DN_HEADS, CHUNK, CHUNK)
        ydn = _dn_scan(qkvc, gb, tmat, amat, main, eg, eb, dn_norm[i][None, :], batch, nc, chunks)

        ssd_vec = _lane_vec([(LANE_DT, ssd_dt_bias[i]), (LANE_DT, ssd_a_log[i])])
        d_exp = jnp.repeat(ssd_d[i].astype(F32), SSD_HEAD_DIM)[None, :]
        yssd = _ssd(main, small, ssd_conv_w[i], ssd_conv_b[i][None, :], ssd_vec, e16, psel, d_exp,
                    ssd_norm[i][None, :], batch, nc, chunks)

        h = _out(yssd, ydn, main, h, sc_conv_w[i], w_branch[i].astype(BF16), w_out[i].astype(BF16),
                 norm_post[i][None, :], batch, tp, rt_out)

    return h.reshape(batch, tp, d)[:, CHUNK:]
```

```python
import functools

import jax
import jax.numpy as jnp
from jax import lax
from jax.experimental import pallas as pl
from jax.experimental.pallas import tpu as pltpu

F32 = jnp.float32
BF16 = jnp.bfloat16

D_MODEL = 1024
N_META = 16
CHUNK = 64
CHUNK_SHIFT = 6
PAD_FRONT = CHUNK - N_META
NORM_EPS = 1e-6
SSD_HEADS = 16
SSD_HEAD_DIM = 64
SSD_GROUPS = 2
SSD_STATE = 128
SSD_XBC = 1536
DN_HEADS = 8
DN_HEAD_DIM = 128
CONV_TAPS = 4
LANES = 128
HALO = 16
CONV_COLS = 256
PROJ_TN = 1536
VMEM_LIMIT = 52 * 1024 * 1024

MAIN_WIDTH = 13824
COL_QKV = 0
COL_GATES = 1
COL_SSD_Z = 6
COL_SC_B = 7
COL_SC_C = 8
COL_SC_H = 9
COL_SC_G = 10
COL_DN_Z = 11
COL_XBC = 8
LANE_DT = 0
LANE_BETA = 16
LANE_A = 24


def _dot(a, b):
    return jnp.dot(a, b, preferred_element_type=F32)


def _dot_nt(a, b):
    return lax.dot_general(a, b, (((1,), (1,)), ((), ())), preferred_element_type=F32)


def _dot_tn(a, b):
    return lax.dot_general(a, b, (((0,), (0,)), ((), ())), preferred_element_type=F32)


def _split3(x):
    hi = x.astype(BF16)
    r1 = x - hi.astype(F32)
    mid = r1.astype(BF16)
    lo = (r1 - mid.astype(F32)).astype(BF16)
    return hi, mid, lo


def _dot_sel(x, sel, terms=3):
    parts = _split3(x)[:terms]
    out = _dot(parts[0], sel)
    for p in parts[1:]:
        out = out + _dot(p, sel)
    return out


def _sel_dot(sel, x):
    hi, mid, lo = _split3(x)
    return _dot(sel, hi) + _dot(sel, mid) + _dot(sel, lo)


def _sel_dot_nt(sel, x):
    hi, mid, lo = _split3(x)
    return _dot_nt(sel, hi) + _dot_nt(sel, mid) + _dot_nt(sel, lo)


def _sigmoid(x):
    return 1.0 / (1.0 + jnp.exp(-x))


def _silu(x):
    return x * _sigmoid(x)


def _softplus(x):
    return jnp.maximum(x, 0.0) + jnp.log(1.0 + jnp.exp(-jnp.abs(x)))


def _iota(shape, dim):
    return lax.broadcasted_iota(jnp.int32, shape, dim)


def _chunk_tril_bf16(rows):
    r = _iota((rows, rows), 0)
    c = _iota((rows, rows), 1)
    same = lax.shift_right_logical(r, CHUNK_SHIFT) == lax.shift_right_logical(c, CHUNK_SHIFT)
    return ((r >= c) & same).astype(BF16)


def _eye_bf16(n=LANES):
    return (_iota((n, n), 0) == _iota((n, n), 1)).astype(BF16)


def _last_row_of_chunks(x, chunks):
    return jnp.concatenate(
        [jnp.broadcast_to(x[(c + 1) * CHUNK - 1:(c + 1) * CHUNK, :], (CHUNK, x.shape[1]))
         for c in range(chunks)], axis=0)


def _stage_rows(x_ref, halo_ref, xe_ref, first):
    rows = x_ref.shape[0]
    halo = halo_ref[...]
    xe_ref[0:HALO, :] = jnp.where(first, jnp.zeros_like(halo), halo)
    xe_ref[HALO:HALO + rows, :] = x_ref[...]


def _conv_cols(xe_ref, smat_ref, w_ref, cs, rows):
    sh = _dot(smat_ref[...], xe_ref[:, cs])
    acc = None
    for k in range(CONV_TAPS):
        term = w_ref[k:k + 1, cs] * sh[k * rows:(k + 1) * rows]
        acc = term if acc is None else acc + term
    return acc


def _in_proj_kernel(x_ref, gain_ref, w_ref, ws_ref, main_ref, small_ref, xn_ref):
    @pl.when(pl.program_id(1) == 0)
    def _():
        x = x_ref[...]
        ms = jnp.mean(x * x, axis=-1, keepdims=True)
        xn = (x * lax.rsqrt(ms + NORM_EPS) * gain_ref[...]).astype(BF16)
        xn_ref[...] = xn
        small_ref[...] = _dot(xn, ws_ref[...])

    main_ref[...] = _dot(xn_ref[...], w_ref[...]).astype(BF16)


def _in_proj(h, gain, w_main, w_small, tm):
    rows = h.shape[0]
    grid = (rows // tm, MAIN_WIDTH // PROJ_TN)
    return pl.pallas_call(
        _in_proj_kernel,
        grid=grid,
        in_specs=[
            pl.BlockSpec((tm, D_MODEL), lambda i, j: (i, 0)),
            pl.BlockSpec((1, D_MODEL), lambda i, j: (0, 0)),
            pl.BlockSpec((D_MODEL, PROJ_TN), lambda i, j: (0, j)),
            pl.BlockSpec((D_MODEL, LANES), lambda i, j: (0, 0)),
        ],
        out_specs=[
            pl.BlockSpec((tm, PROJ_TN), lambda i, j: (i, j)),
            pl.BlockSpec((tm, LANES), lambda i, j: (i, 0)),
        ],
        out_shape=[
            jax.ShapeDtypeStruct((rows, MAIN_WIDTH), BF16),
            jax.ShapeDtypeStruct((rows, LANES), F32),
        ],
        scratch_shapes=[pltpu.VMEM((tm, D_MODEL), BF16)],
        compiler_params=pltpu.CompilerParams(
            dimension_semantics=("arbitrary", "arbitrary"), vmem_limit_bytes=VMEM_LIMIT),
        name="in_proj",
    )(h, gain, w_main, w_small)


def _row_specs(rt, nt):
    def tile(width, col):
        return pl.BlockSpec((rt, width), lambda b, t: (b * nt + t, col))

    def halo(width, col):
        return pl.BlockSpec(
            (HALO, width), lambda b, t: (jnp.maximum((b * nt + t) * (rt // HALO) - 1, 0), col))

    def const(shape):
        return pl.BlockSpec(shape, lambda b, t: (0,) * len(shape), pipeline_mode=pl.Buffered(1))

    def inst(*tail):
        shape = (rt // CHUNK, DN_HEADS) + tail
        return pl.BlockSpec(shape, lambda b, t: (b * nt + t,) + (0,) * (len(shape) - 1))

    return tile, halo, const, inst


def _dn_prep_kernel(qkv_ref, halo_ref, small_ref, cw_ref, vec_ref, eg_ref, eb_ref, smat_ref,
                    qt_ref, rhs_ref, kd_ref, l_ref, a_ref, egl_ref, xe_ref, *, chunks):
    t = pl.program_id(1)
    rows = chunks * CHUNK
    _stage_rows(qkv_ref, halo_ref, xe_ref, t == 0)

    sm = small_ref[...]
    lane = _iota(sm.shape, 1)
    real = (t * rows + _iota(sm.shape, 0)) >= PAD_FRONT
    is_beta = (lane >= LANE_BETA) & (lane < LANE_BETA + DN_HEADS)
    is_a = (lane >= LANE_A) & (lane < LANE_A + DN_HEADS)
    g = -jnp.exp(vec_ref[1:2, :]) * _softplus(sm + vec_ref[0:1, :])
    g = jnp.where(real & is_a, g, 0.0)
    beta = jnp.where(real & is_beta, _sigmoid(sm), 0.0)
    gcum = _sel_dot(_chunk_tril_bf16(rows), g)
    ge = _dot_sel(gcum, eg_ref[...])
    be = _dot_sel(beta, eb_ref[...], terms=2)
    gl = _last_row_of_chunks(ge, chunks)
    eg = jnp.exp(ge)
    ekd = jnp.exp(gl - ge)

    heads_per_block = CONV_COLS // DN_HEAD_DIM
    qkv = []
    for j in range(3 * D_MODEL // CONV_COLS):
        cs = slice(j * CONV_COLS, (j + 1) * CONV_COLS)
        val = _silu(_conv_cols(xe_ref, smat_ref, cw_ref, cs, rows))
        for i in range(heads_per_block):
            vh = val[:, i * DN_HEAD_DIM:(i + 1) * DN_HEAD_DIM]
            if j < 2 * D_MODEL // CONV_COLS:
                scale = DN_HEAD_DIM ** -0.5 if j < D_MODEL // CONV_COLS else 1.0
                ss = jnp.sum(vh * vh, axis=-1, keepdims=True)
                vh = vh * (lax.rsqrt(ss + NORM_EPS) * scale)
            qkv.append(vh)
    qs, ks, vs = qkv[:DN_HEADS], qkv[DN_HEADS:2 * DN_HEADS], qkv[2 * DN_HEADS:]

    for h in range(DN_HEADS):
        hs = slice(h * DN_HEAD_DIM, (h + 1) * DN_HEAD_DIM)
        bt = be[:, hs]
        qt_ref[:, hs] = (qs[h] * eg[:, hs]).astype(BF16)
        rhs_ref[:, 2 * h * DN_HEAD_DIM:(2 * h + 1) * DN_HEAD_DIM] = (vs[h] * bt).astype(BF16)
        rhs_ref[:, (2 * h + 1) * DN_HEAD_DIM:(2 * h + 2) * DN_HEAD_DIM] = (ks[h] * (bt * eg[:, hs])).astype(BF16)
        kd_ref[:, hs] = (ks[h] * ekd[:, hs]).astype(BF16)

    li = _iota((CHUNK, CHUNK), 0)
    si = _iota((CHUNK, CHUNK), 1)
    causal = li >= si
    strict = li > si
    eye = _eye_bf16()
    egl = jnp.exp(gl)
    for c in range(chunks):
        rc = slice(c * CHUNK, (c + 1) * CHUNK)
        gct = _sel_dot_nt(eye, gcum[rc])
        for h in range(DN_HEADS):
            c0 = h * DN_HEAD_DIM
            kb = ks[h][rc].astype(BF16)
            kq = _dot_nt(jnp.concatenate([kb, qs[h][rc].astype(BF16)], axis=0), kb)
            gs = gct[LANE_A + h:LANE_A + h + 1, :]
            decay = jnp.where(causal, jnp.exp(jnp.minimum(ge[rc, c0:c0 + CHUNK] - gs, 0.0)), 0.0)
            l_ref[c, h] = jnp.where(strict, be[rc, c0:c0 + CHUNK] * kq[:CHUNK] * decay, 0.0)
            a_ref[c, h] = (kq[CHUNK:] * decay).astype(BF16)
            egl_ref[c, h:h + 1, :] = egl[rc.stop - 1:rc.stop, c0:c0 + LANES]


def _dn_prep(main, small, conv_w, vec, eg, eb, smat, batch, nc, chunks):
    rows = main.shape[0]
    nt = nc // chunks
    rt = chunks * CHUNK
    tile, halo, const, inst = _row_specs(rt, nt)
    return pl.pallas_call(
        functools.partial(_dn_prep_kernel, chunks=chunks),
        grid=(batch, nt),
        in_specs=[
            tile(3 * D_MODEL, COL_QKV), halo(3 * D_MODEL, COL_QKV), tile(LANES, 0),
            const((CONV_TAPS, 3 * D_MODEL)), const((8, LANES)),
            const((LANES, D_MODEL)), const((LANES, D_MODEL)), const(smat.shape),
        ],
        out_specs=[
            tile(D_MODEL, 0), tile(2 * D_MODEL, 0), tile(D_MODEL, 0),
            inst(CHUNK, CHUNK), inst(CHUNK, CHUNK), inst(LANES),
        ],
        out_shape=[
            jax.ShapeDtypeStruct((rows, D_MODEL), BF16),
            jax.ShapeDtypeStruct((rows, 2 * D_MODEL), BF16),
            jax.ShapeDtypeStruct((rows, D_MODEL), BF16),
            jax.ShapeDtypeStruct((batch * nc, DN_HEADS, CHUNK, CHUNK), F32),
            jax.ShapeDtypeStruct((batch * nc, DN_HEADS, CHUNK, CHUNK), BF16),
            jax.ShapeDtypeStruct((batch * nc, DN_HEADS, LANES), F32),
        ],
        scratch_shapes=[pltpu.VMEM((HALO + rt, 3 * D_MODEL), BF16)],
        compiler_params=pltpu.CompilerParams(
            dimension_semantics=("arbitrary", "arbitrary"), vmem_limit_bytes=VMEM_LIMIT),
        name="dn_prep",
    )(main, main, small, conv_w, vec, eg, eb, smat)


def _dn_solve_kernel(l_ref, t_ref):
    col = _iota((CHUNK, LANES), 0)
    sub = 8

    def init_body(i, carry):
        t_ref[i] = (col == i).astype(F32)
        return carry

    lax.fori_loop(0, CHUNK, init_body, 0)

    def row_body(i, carry):
        def inner(jb, acc):
            j0 = pl.multiple_of(jb * sub, sub)
            blk = l_ref[i, pl.ds(j0, sub), :]
            for jj in range(sub):
                acc = acc - blk[jj:jj + 1, :] * t_ref[j0 + jj]
            return acc

        nblk = lax.shift_right_logical(i + (sub - 1), 3)
        t_ref[i] = lax.fori_loop(0, nblk, inner, (col == i).astype(F32))
        return carry

    lax.fori_loop(1, CHUNK, row_body, 0)


def _dn_solve(lt):
    ni = lt.shape[-1]
    spec = pl.BlockSpec((CHUNK, CHUNK, LANES), lambda i: (0, 0, i))
    return pl.pallas_call(
        _dn_solve_kernel,
        grid=(pl.cdiv(ni, LANES),),
        in_specs=[spec],
        out_specs=spec,
        out_shape=jax.ShapeDtypeStruct(lt.shape, F32),
        compiler_params=pltpu.CompilerParams(dimension_semantics=("arbitrary",)),
        name="dn_solve",
    )(lt)


def _dn_scan_kernel(qt_ref, rhs_ref, kd_ref, t_ref, a_ref, egl_ref, z_ref, nw_ref, y_ref, s_ref,
                    *, chunks):
    @pl.when(pl.program_id(1) == 0)
    def _():
        s_ref[...] = jnp.zeros_like(s_ref)

    nw = nw_ref[...]
    heads = range(DN_HEADS)
    uw = [[_dot(t_ref[c, h], rhs_ref[c * CHUNK:(c + 1) * CHUNK, 2 * h * DN_HEAD_DIM:(2 * h + 2) * DN_HEAD_DIM])
           for h in heads] for c in range(chunks)]
    s = [s_ref[h] for h in heads]
    for c in range(chunks):
        rc = slice(c * CHUNK, (c + 1) * CHUNK)
        ws = []
        for h in heads:
            hs = slice(h * DN_HEAD_DIM, (h + 1) * DN_HEAD_DIM)
            wq = jnp.concatenate([uw[c][h][:, DN_HEAD_DIM:].astype(BF16), qt_ref[rc, hs]], axis=0)
            ws.append(_dot(wq, s[h].astype(BF16)))
        for h in heads:
            hs = slice(h * DN_HEAD_DIM, (h + 1) * DN_HEAD_DIM)
            vn = (uw[c][h][:, :DN_HEAD_DIM] - ws[h][:CHUNK]).astype(BF16)
            o = ws[h][CHUNK:] + _dot(a_ref[c, h], vn)
            s[h] = s[h] * egl_ref[c, h:h + 1, :] + _dot_tn(kd_ref[rc, hs], vn)
            ms = jnp.mean(o * o, axis=-1, keepdims=True)
            on = o * lax.rsqrt(ms + NORM_EPS) * nw
            y_ref[rc, hs] = (on * _silu(z_ref[rc, hs].astype(F32))).astype(BF16)
    for h in heads:
        s_ref[h] = s[h]


def _dn_scan(qt, rhs, kd, tmat, amat, egl, main, norm_w, batch, nc, chunks):
    rows = qt.shape[0]
    nt = nc // chunks
    rt = chunks * CHUNK
    tile, _, const, inst = _row_specs(rt, nt)
    return pl.pallas_call(
        functools.partial(_dn_scan_kernel, chunks=chunks),
        grid=(batch, nt),
        in_specs=[
            tile(D_MODEL, 0), tile(2 * D_MODEL, 0), tile(D_MODEL, 0), inst(CHUNK, CHUNK),
            inst(CHUNK, CHUNK), inst(LANES), tile(D_MODEL, COL_DN_Z), const((1, DN_HEAD_DIM)),
        ],
        out_specs=tile(D_MODEL, 0),
        out_shape=jax.ShapeDtypeStruct((rows, D_MODEL), BF16),
        scratch_shapes=[pltpu.VMEM((DN_HEADS, DN_HEAD_DIM, DN_HEAD_DIM), F32)],
        compiler_params=pltpu.CompilerParams(
            dimension_semantics=("arbitrary", "arbitrary"), vmem_limit_bytes=VMEM_LIMIT),
        name="dn_scan",
    )(qt, rhs, kd, tmat, amat, egl, main, norm_w)


def _ssd_kernel(xbc_ref, halo_ref, z_ref, small_ref, cw_ref, cb_ref, vec_ref, e16_ref, psel_ref,
                de_ref, nw_ref, smat_ref, y_ref, xe_ref, h_ref, *, chunks):
    t = pl.program_id(1)
    rows = chunks * CHUNK

    @pl.when(t == 0)
    def _():
        h_ref[...] = jnp.zeros_like(h_ref)

    _stage_rows(xbc_ref, halo_ref, xe_ref, t == 0)

    sm = small_ref[...]
    lane = _iota(sm.shape, 1)
    real = (t * rows + _iota(sm.shape, 0)) >= PAD_FRONT
    dt = jnp.where(real & (lane < SSD_HEADS), _softplus(sm + vec_ref[0:1, :]), 0.0)
    acum = _sel_dot(_chunk_tril_bf16(rows), dt * (-jnp.exp(vec_ref[1:2, :])))
    ae = _dot_sel(acum, e16_ref[...])
    dte = _dot_sel(dt, e16_ref[...], terms=2)
    alast = _last_row_of_chunks(ae, chunks)

    conv = []
    for j in range(SSD_XBC // CONV_COLS):
        cs = slice(j * CONV_COLS, (j + 1) * CONV_COLS)
        conv.append(_silu(_conv_cols(xe_ref, smat_ref, cw_ref, cs, rows) + cb_ref[:, cs]))
    nx = D_MODEL // CONV_COLS
    xs = jnp.concatenate(conv[:nx], axis=1)
    bc = jnp.concatenate(conv[nx:], axis=1).astype(BF16)
    xc = xs * dte
    xcb = xc.astype(BF16)
    xdec = (xc * jnp.exp(alast - ae)).astype(BF16)
    eae = jnp.exp(ae)
    ealast = jnp.exp(alast)

    half = SSD_HEADS // SSD_GROUPS * SSD_HEAD_DIM
    pairs = SSD_HEADS // SSD_GROUPS // 2
    cr = [slice(c * CHUNK, (c + 1) * CHUNK) for c in range(chunks)]
    gsl = [slice(g * half, (g + 1) * half) for g in range(SSD_GROUPS)]
    bgs = [[bc[cr[c], g * SSD_STATE:(g + 1) * SSD_STATE] for g in range(SSD_GROUPS)] for c in range(chunks)]
    cgs = [[bc[cr[c], (SSD_GROUPS + g) * SSD_STATE:(SSD_GROUPS + g + 1) * SSD_STATE]
            for g in range(SSD_GROUPS)] for c in range(chunks)]

    states = [[_dot_tn(bgs[c][g], xdec[cr[c], gsl[g]]) for g in range(SSD_GROUPS)] for c in range(chunks)]
    yoff = [[None] * SSD_GROUPS for _ in range(chunks)]
    for g in range(SSD_GROUPS):
        hg = h_ref[g]
        for c in range(chunks):
            yoff[c][g] = _dot(cgs[c][g], hg.astype(BF16)) * eae[cr[c], gsl[g]]
            hg = hg * ealast[cr[c].stop - 1:cr[c].stop, gsl[g]] + states[c][g]
        h_ref[g] = hg

    psel = psel_ref[...]
    prow = _iota((2 * CHUNK, LANES), 0)
    plane = _iota((2 * CHUNK, LANES), 1)
    keep = (prow < CHUNK) == ((plane & 1) == 0)
    bd_mask = (prow < CHUNK) == (plane < CHUNK)
    causal2 = _iota((CHUNK, LANES), 0) >= (_iota((CHUNK, LANES), 1) & (CHUNK - 1))
    cbcb = [[_dot_nt(cgs[c][g], jnp.concatenate([bgs[c][g], bgs[c][g]], axis=0))
             for g in range(SSD_GROUPS)] for c in range(chunks)]
    rowpair = []
    for c in range(chunks):
        ac = acum[cr[c]]
        a2 = jnp.where(keep, jnp.concatenate([ac, ac], axis=0), 0.0)
        rowpair.append(_sel_dot_nt(psel, a2))
    nw = nw_ref[...]
    de = de_ref[...]
    for c in range(chunks):
        ys = []
        for g in range(SSD_GROUPS):
            for p in range(pairs):
                j = g * pairs + p
                js = slice(j * LANES, (j + 1) * LANES)
                diff = ae[cr[c], js] - rowpair[c][j:j + 1, :]
                m = jnp.where(causal2, cbcb[c][g] * jnp.exp(jnp.minimum(diff, 0.0)), 0.0).astype(BF16)
                xp = xcb[cr[c], js]
                bd = jnp.where(bd_mask, jnp.concatenate([xp, xp], axis=0), jnp.zeros((), BF16))
                ys.append(_dot(m, bd) + yoff[c][g][:, p * LANES:(p + 1) * LANES])
        y = jnp.concatenate(ys, axis=1) + de * xs[cr[c]]
        y = y * _silu(z_ref[cr[c], :].astype(F32))
        for g in range(SSD_GROUPS):
            yg = y[:, gsl[g]]
            ms = jnp.mean(yg * yg, axis=-1, keepdims=True)
            y_ref[cr[c], gsl[g]] = (yg * lax.rsqrt(ms + NORM_EPS) * nw[:, gsl[g]]).astype(BF16)


def _ssd(main, small, conv_w, conv_b, vec, e16, psel, d_exp, norm_w, smat, batch, nc, chunks):
    rows = main.shape[0]
    nt = nc // chunks
    rt = chunks * CHUNK
    tile, halo, const, _ = _row_specs(rt, nt)
    return pl.pallas_call(
        functools.partial(_ssd_kernel, chunks=chunks),
        grid=(batch, nt),
        in_specs=[
            tile(SSD_XBC, COL_XBC), halo(SSD_XBC, COL_XBC), tile(D_MODEL, COL_SSD_Z), tile(LANES, 0),
            const((CONV_TAPS, SSD_XBC)), const((1, SSD_XBC)), const((8, LANES)),
            const((LANES, D_MODEL)), const((8, LANES)), const((1, D_MODEL)), const((1, D_MODEL)),
            const(smat.shape),
        ],
        out_specs=tile(D_MODEL, 0),
        out_shape=jax.ShapeDtypeStruct((rows, D_MODEL), BF16),
        scratch_shapes=[
            pltpu.VMEM((HALO + rt, SSD_XBC), BF16),
            pltpu.VMEM((SSD_GROUPS, SSD_STATE, D_MODEL // SSD_GROUPS), F32),
        ],
        compiler_params=pltpu.CompilerParams(
            dimension_semantics=("arbitrary", "arbitrary"), vmem_limit_bytes=VMEM_LIMIT),
        name="ssd",
    )(main, main, main, small, conv_w, conv_b, vec, e16, psel, d_exp, norm_w, smat)


def _out_kernel(yssd_ref, ydn_ref, b_ref, c_ref, chalo_ref, h_ref, hhalo_ref, g_ref, gates_ref,
                x_ref, scw_ref, wb_ref, wo_ref, np_ref, o_ref, upad_ref):
    t = pl.program_id(1)
    rows = x_ref.shape[0]
    first = t == 0
    uh = chalo_ref[...].astype(F32) * hhalo_ref[...].astype(F32)
    upad_ref[0:HALO, :] = jnp.where(first, 0.0, uh)
    upad_ref[HALO:HALO + rows, :] = c_ref[...].astype(F32) * h_ref[...].astype(F32)
    conv = None
    for k in range(3):
        term = scw_ref[k:k + 1, :] * upad_ref[pl.ds(HALO - 2 + k, rows), :]
        conv = term if conv is None else conv + term
    ysc = (b_ref[...].astype(F32) * conv * _silu(g_ref[...].astype(F32))).astype(BF16)

    merged = None
    for n, y in enumerate((yssd_ref[...], ysc, ydn_ref[...])):
        gate = _sigmoid(gates_ref[:, n * D_MODEL:(n + 1) * D_MODEL].astype(F32))
        term = gate * _dot(y, wb_ref[n])
        merged = term if merged is None else merged + term
    out = _dot(merged.astype(BF16), wo_ref[...])
    ms = jnp.mean(out * out, axis=-1, keepdims=True)
    res = x_ref[...] + out * lax.rsqrt(ms + NORM_EPS) * np_ref[...]
    real = (t * rows + _iota(res.shape, 0)) >= PAD_FRONT
    o_ref[...] = jnp.where(real, res, 0.0)


def _out(yssd, ydn, main, x, sc_w, w_branch, w_out, norm_post, batch, tp, rt):
    rows = x.shape[0]
    nt = tp // rt
    tile, halo, const, _ = _row_specs(rt, nt)
    return pl.pallas_call(
        _out_kernel,
        grid=(batch, nt),
        in_specs=[
            tile(D_MODEL, 0), tile(D_MODEL, 0),
            tile(D_MODEL, COL_SC_B), tile(D_MODEL, COL_SC_C), halo(D_MODEL, COL_SC_C),
            tile(D_MODEL, COL_SC_H), halo(D_MODEL, COL_SC_H), tile(D_MODEL, COL_SC_G),
            tile(3 * D_MODEL, COL_GATES), tile(D_MODEL, 0),
            const((3, D_MODEL)), const((3, D_MODEL, D_MODEL)), const((D_MODEL, D_MODEL)),
            const((1, D_MODEL)),
        ],
        out_specs=tile(D_MODEL, 0),
        out_shape=jax.ShapeDtypeStruct((rows, D_MODEL), F32),
        scratch_shapes=[pltpu.VMEM((HALO + rt, D_MODEL), F32)],
        compiler_params=pltpu.CompilerParams(
            dimension_semantics=("arbitrary", "arbitrary"), vmem_limit_bytes=VMEM_LIMIT),
        name="out_proj",
    )(yssd, ydn, main, main, main, main, main, main, main, x, sc_w, w_branch, w_out, norm_post)


def _lane_vec(rows):
    out = jnp.zeros((8, LANES), F32)
    for i, (off, v) in enumerate(rows):
        out = out.at[i, off:off + v.shape[0]].set(v.astype(F32))
    return out


def _expand_matrix(lane0, heads, width):
    r = jnp.arange(LANES)[:, None]
    c = jnp.arange(heads * width)[None, :]
    return (r == lane0 + c // width).astype(BF16)


def _shift_matrix(rows):
    r = jnp.arange(CONV_TAPS * rows)[:, None]
    c = jnp.arange(HALO + rows)[None, :]
    return (c == HALO - (CONV_TAPS - 1) + r % rows + r // rows).astype(BF16)


def _split_w_in(w):
    sizes = (1024, 1536, 16, 1024, 1024, 1024, 1024, 3072, 1024, 8, 8, 3072)
    offs = [0]
    for s in sizes:
        offs.append(offs[-1] + s)
    (ssd_z, xbc, ssd_dt, sc_b, sc_c, sc_h, sc_g, qkv, dn_z, dn_b, dn_a, gates) = [
        w[:, offs[i]:offs[i + 1]] for i in range(len(sizes))]
    main = jnp.concatenate([qkv, gates, ssd_z, sc_b, sc_c, sc_h, sc_g, dn_z, xbc], axis=1).astype(BF16)
    small = jnp.concatenate(
        [ssd_dt, dn_b, dn_a, jnp.zeros((w.shape[0], LANES - 32), w.dtype)], axis=1).astype(BF16)
    return main, small


def kernel(x, meta_tokens, norm_pre, norm_post, w_in, ssd_conv_w, ssd_conv_b, ssd_dt_bias, ssd_a_log,
           ssd_d, ssd_norm, sc_conv_w, dn_conv_w, dn_dt_bias, dn_a_log, dn_norm, w_branch, w_out):
    batch, seq, d = x.shape
    assert d == D_MODEL and seq % CHUNK == 0
    tp = CHUNK + seq
    nc = tp // CHUNK
    chunks = 3 if nc % 3 == 0 else 1
    rows = batch * tp
    tm = tp // 2 if tp % 32 == 0 else tp
    rt_out = tp // 4 if tp % (4 * HALO) == 0 else tp
    depth = w_in.shape[0]

    meta = jnp.broadcast_to(meta_tokens.astype(x.dtype)[None], (batch, N_META, d))
    h = jnp.concatenate([jnp.zeros((batch, PAD_FRONT, d), x.dtype), meta, x], axis=1).reshape(rows, d)

    e16 = _expand_matrix(LANE_DT, SSD_HEADS, SSD_HEAD_DIM)
    eg = _expand_matrix(LANE_A, DN_HEADS, DN_HEAD_DIM)
    eb = _expand_matrix(LANE_BETA, DN_HEADS, DN_HEAD_DIM)
    smat = _shift_matrix(chunks * CHUNK)
    pj = jnp.arange(8)[:, None]
    pc = jnp.arange(LANES)[None, :]
    psel = ((pc // 2 == pj) & (pc < SSD_HEADS)).astype(BF16)

    for i in range(depth):
        w_main, w_small = _split_w_in(w_in[i])
        main, small = _in_proj(h, norm_pre[i][None, :], w_main, w_small, tm)

        dn_vec = _lane_vec([(LANE_A, dn_dt_bias[i]), (LANE_A, dn_a_log[i])])
        qt, rhs, kd, lmat, amat, egl = _dn_prep(
            main, small, dn_conv_w[i], dn_vec, eg, eb, smat, batch, nc, chunks)
        ni = batch * nc * DN_HEADS
        lt = jnp.transpose(lmat.reshape(ni, CHUNK, CHUNK), (1, 2, 0))
        lt = jnp.pad(lt, ((0, 0), (0, 0), (0, -ni % LANES)))
        tmat = jnp.transpose(_dn_solve(lt)[:, :, :ni], (2, 0, 1)).astype(BF16).reshape(
            batch * nc, DN_HEADS, CHUNK, CHUNK)
        ydn = _dn_scan(qt, rhs, kd, tmat, amat, egl, main, dn_norm[i][None, :], batch, nc, chunks)

        ssd_vec = _lane_vec([(LANE_DT, ssd_dt_bias[i]), (LANE_DT, ssd_a_log[i])])
        d_exp = jnp.repeat(ssd_d[i].astype(F32), SSD_HEAD_DIM)[None, :]
        yssd = _ssd(main, small, ssd_conv_w[i], ssd_conv_b[i][None, :], ssd_vec, e16, psel, d_exp,
                    ssd_norm[i][None, :], smat, batch, nc, chunks)

        h = _out(yssd, ydn, main, h, sc_conv_w[i], w_branch[i].astype(BF16), w_out[i].astype(BF16),
                 norm_post[i][None, :], batch, tp, rt_out)

    return h.reshape(batch, tp, d)[:, CHUNK:]
```

```python
import functools

import jax
import jax.numpy as jnp
from jax import lax
from jax.experimental import pallas as pl
from jax.experimental.pallas import tpu as pltpu

F32 = jnp.float32
BF16 = jnp.bfloat16

D_MODEL = 1024
N_META = 16
CHUNK = 64
CHUNK_SHIFT = 6
PAD_FRONT = CHUNK - N_META
NORM_EPS = 1e-6
SSD_HEADS = 16
SSD_HEAD_DIM = 64
SSD_GROUPS = 2
SSD_STATE = 128
SSD_XBC = 1536
DN_HEADS = 8
DN_HEAD_DIM = 128
CONV_TAPS = 4
LANES = 128
HALO = 16
CONV_COLS = 256
PROJ_TN = 1536
VMEM_LIMIT = 52 * 1024 * 1024

MAIN_WIDTH = 13824
COL_QKV = 0
COL_GATES = 1
COL_SSD_Z = 6
COL_SC_G = 7
COL_DN_Z = 8
COL_SC_B = 9
COL_SC_C = 10
COL_SC_H = 11
COL_XBC = 8
LANE_DT = 0
LANE_BETA = 16
LANE_A = 24


def _dot(a, b):
    return jnp.dot(a, b, preferred_element_type=F32)


def _dot_nt(a, b):
    return lax.dot_general(a, b, (((1,), (1,)), ((), ())), preferred_element_type=F32)


def _dot_tn(a, b):
    return lax.dot_general(a, b, (((0,), (0,)), ((), ())), preferred_element_type=F32)


def _split3(x):
    hi = x.astype(BF16)
    r1 = x - hi.astype(F32)
    mid = r1.astype(BF16)
    lo = (r1 - mid.astype(F32)).astype(BF16)
    return hi, mid, lo


def _dot_sel(x, sel, terms=3):
    parts = _split3(x)[:terms]
    out = _dot(parts[0], sel)
    for p in parts[1:]:
        out = out + _dot(p, sel)
    return out


def _sel_dot(sel, x):
    hi, mid, lo = _split3(x)
    return _dot(sel, hi) + _dot(sel, mid) + _dot(sel, lo)


def _sel_dot_nt(sel, x):
    hi, mid, lo = _split3(x)
    return _dot_nt(sel, hi) + _dot_nt(sel, mid) + _dot_nt(sel, lo)


def _sigmoid(x):
    return 1.0 / (1.0 + jnp.exp(-x))


def _silu(x):
    return x * _sigmoid(x)


def _softplus(x):
    return jnp.maximum(x, 0.0) + jnp.log(1.0 + jnp.exp(-jnp.abs(x)))


def _iota(shape, dim):
    return lax.broadcasted_iota(jnp.int32, shape, dim)


def _chunk_tril_bf16(rows):
    r = _iota((rows, rows), 0)
    c = _iota((rows, rows), 1)
    same = lax.shift_right_logical(r, CHUNK_SHIFT) == lax.shift_right_logical(c, CHUNK_SHIFT)
    return ((r >= c) & same).astype(BF16)


def _eye_bf16(n=LANES):
    return (_iota((n, n), 0) == _iota((n, n), 1)).astype(BF16)


def _last_row_of_chunks(x, chunks):
    return jnp.concatenate(
        [jnp.broadcast_to(x[(c + 1) * CHUNK - 1:(c + 1) * CHUNK, :], (CHUNK, x.shape[1]))
         for c in range(chunks)], axis=0)


def _stage_rows(x_ref, halo_ref, xe_ref, first):
    rows = x_ref.shape[0]
    halo = halo_ref[...]
    xe_ref[0:HALO, :] = jnp.where(first, jnp.zeros_like(halo), halo)
    xe_ref[HALO:HALO + rows, :] = x_ref[...]


def _conv_cols(xe_ref, smat_ref, w_ref, cs, rows):
    sh = _dot(smat_ref[...], xe_ref[:, cs])
    acc = None
    for k in range(CONV_TAPS):
        term = w_ref[k:k + 1, cs] * sh[k * rows:(k + 1) * rows]
        acc = term if acc is None else acc + term
    return acc


def _in_proj_kernel(x_ref, gain_ref, w_ref, ws_ref, main_ref, small_ref, xn_ref):
    @pl.when(pl.program_id(1) == 0)
    def _():
        x = x_ref[...]
        ms = jnp.mean(x * x, axis=-1, keepdims=True)
        xn = (x * lax.rsqrt(ms + NORM_EPS) * gain_ref[...]).astype(BF16)
        xn_ref[...] = xn
        small_ref[...] = _dot(xn, ws_ref[...])

    main_ref[...] = _dot(xn_ref[...], w_ref[...]).astype(BF16)


def _in_proj(h, gain, w_main, w_small, tm):
    rows = h.shape[0]
    grid = (rows // tm, MAIN_WIDTH // PROJ_TN)
    return pl.pallas_call(
        _in_proj_kernel,
        grid=grid,
        in_specs=[
            pl.BlockSpec((tm, D_MODEL), lambda i, j: (i, 0)),
            pl.BlockSpec((1, D_MODEL), lambda i, j: (0, 0)),
            pl.BlockSpec((D_MODEL, PROJ_TN), lambda i, j: (0, j)),
            pl.BlockSpec((D_MODEL, LANES), lambda i, j: (0, 0)),
        ],
        out_specs=[
            pl.BlockSpec((tm, PROJ_TN), lambda i, j: (i, j)),
            pl.BlockSpec((tm, LANES), lambda i, j: (i, 0)),
        ],
        out_shape=[
            jax.ShapeDtypeStruct((rows, MAIN_WIDTH), BF16),
            jax.ShapeDtypeStruct((rows, LANES), F32),
        ],
        scratch_shapes=[pltpu.VMEM((tm, D_MODEL), BF16)],
        compiler_params=pltpu.CompilerParams(
            dimension_semantics=("arbitrary", "arbitrary"), vmem_limit_bytes=VMEM_LIMIT),
        name="in_proj",
    )(h, gain, w_main, w_small)


def _row_specs(rt, nt):
    def tile(width, col):
        return pl.BlockSpec((rt, width), lambda b, t: (b * nt + t, col))

    def halo(width, col):
        return pl.BlockSpec(
            (HALO, width), lambda b, t: (jnp.maximum((b * nt + t) * (rt // HALO) - 1, 0), col))

    def const(shape):
        return pl.BlockSpec(shape, lambda b, t: (0,) * len(shape), pipeline_mode=pl.Buffered(1))

    def inst(*tail):
        shape = (rt // CHUNK, DN_HEADS) + tail
        return pl.BlockSpec(shape, lambda b, t: (b * nt + t,) + (0,) * (len(shape) - 1))

    return tile, halo, const, inst


def _dn_prep_kernel(qkv_ref, halo_ref, small_ref, cw_ref, vec_ref, eg_ref, eb_ref, smat_ref,
                    qt_ref, rhs_ref, kd_ref, l_ref, a_ref, egl_ref, xe_ref, *, chunks):
    t = pl.program_id(1)
    rows = chunks * CHUNK
    _stage_rows(qkv_ref, halo_ref, xe_ref, t == 0)

    sm = small_ref[...]
    lane = _iota(sm.shape, 1)
    real = (t * rows + _iota(sm.shape, 0)) >= PAD_FRONT
    is_beta = (lane >= LANE_BETA) & (lane < LANE_BETA + DN_HEADS)
    is_a = (lane >= LANE_A) & (lane < LANE_A + DN_HEADS)
    g = -jnp.exp(vec_ref[1:2, :]) * _softplus(sm + vec_ref[0:1, :])
    g = jnp.where(real & is_a, g, 0.0)
    beta = jnp.where(real & is_beta, _sigmoid(sm), 0.0)
    gcum = _sel_dot(_chunk_tril_bf16(rows), g)
    ge = _dot_sel(gcum, eg_ref[...])
    be = _dot_sel(beta, eb_ref[...], terms=2)
    gl = _last_row_of_chunks(ge, chunks)
    eg = jnp.exp(ge)
    ekd = jnp.exp(gl - ge)

    heads_per_block = CONV_COLS // DN_HEAD_DIM
    qkv = []
    for j in range(3 * D_MODEL // CONV_COLS):
        cs = slice(j * CONV_COLS, (j + 1) * CONV_COLS)
        val = _silu(_conv_cols(xe_ref, smat_ref, cw_ref, cs, rows))
        for i in range(heads_per_block):
            vh = val[:, i * DN_HEAD_DIM:(i + 1) * DN_HEAD_DIM]
            if j < 2 * D_MODEL // CONV_COLS:
                scale = DN_HEAD_DIM ** -0.5 if j < D_MODEL // CONV_COLS else 1.0
                ss = jnp.sum(vh * vh, axis=-1, keepdims=True)
                vh = vh * (lax.rsqrt(ss + NORM_EPS) * scale)
            qkv.append(vh)
    qs, ks, vs = qkv[:DN_HEADS], qkv[DN_HEADS:2 * DN_HEADS], qkv[2 * DN_HEADS:]

    for h in range(DN_HEADS):
        hs = slice(h * DN_HEAD_DIM, (h + 1) * DN_HEAD_DIM)
        bt = be[:, hs]
        qt_ref[:, hs] = (qs[h] * eg[:, hs]).astype(BF16)
        rhs_ref[:, 2 * h * DN_HEAD_DIM:(2 * h + 1) * DN_HEAD_DIM] = (vs[h] * bt).astype(BF16)
        rhs_ref[:, (2 * h + 1) * DN_HEAD_DIM:(2 * h + 2) * DN_HEAD_DIM] = (ks[h] * (bt * eg[:, hs])).astype(BF16)
        kd_ref[:, hs] = (ks[h] * ekd[:, hs]).astype(BF16)

    li = _iota((CHUNK, CHUNK), 0)
    si = _iota((CHUNK, CHUNK), 1)
    causal = li >= si
    strict = li > si
    eye = _eye_bf16()
    egl = [jnp.exp(ge[(c + 1) * CHUNK - 1:(c + 1) * CHUNK, :]) for c in range(chunks)]
    for c in range(chunks):
        rc = slice(c * CHUNK, (c + 1) * CHUNK)
        gct = _sel_dot_nt(eye, gcum[rc])
        for h in range(DN_HEADS):
            c0 = h * DN_HEAD_DIM
            kb = ks[h][rc].astype(BF16)
            kq = _dot_nt(jnp.concatenate([kb, qs[h][rc].astype(BF16)], axis=0), kb)
            gs = gct[LANE_A + h:LANE_A + h + 1, :]
            decay = jnp.where(causal, jnp.exp(jnp.minimum(ge[rc, c0:c0 + CHUNK] - gs, 0.0)), 0.0)
            l_ref[c, h] = jnp.where(strict, be[rc, c0:c0 + CHUNK] * kq[:CHUNK] * decay, 0.0)
            a_ref[c, h] = (kq[CHUNK:] * decay).astype(BF16)
            egl_ref[c, h:h + 1, :] = egl[c][:, c0:c0 + LANES]


def _dn_prep(main, small, conv_w, vec, eg, eb, smat, batch, nc, chunks):
    rows = main.shape[0]
    nt = nc // chunks
    rt = chunks * CHUNK
    tile, halo, const, inst = _row_specs(rt, nt)
    return pl.pallas_call(
        functools.partial(_dn_prep_kernel, chunks=chunks),
        grid=(batch, nt),
        in_specs=[
            tile(3 * D_MODEL, COL_QKV), halo(3 * D_MODEL, COL_QKV), tile(LANES, 0),
            const((CONV_TAPS, 3 * D_MODEL)), const((8, LANES)),
            const((LANES, D_MODEL)), const((LANES, D_MODEL)), const(smat.shape),
        ],
        out_specs=[
            tile(D_MODEL, 0), tile(2 * D_MODEL, 0), tile(D_MODEL, 0),
            inst(CHUNK, CHUNK), inst(CHUNK, CHUNK), inst(LANES),
        ],
        out_shape=[
            jax.ShapeDtypeStruct((rows, D_MODEL), BF16),
            jax.ShapeDtypeStruct((rows, 2 * D_MODEL), BF16),
            jax.ShapeDtypeStruct((rows, D_MODEL), BF16),
            jax.ShapeDtypeStruct((batch * nc, DN_HEADS, CHUNK, CHUNK), F32),
            jax.ShapeDtypeStruct((batch * nc, DN_HEADS, CHUNK, CHUNK), BF16),
            jax.ShapeDtypeStruct((batch * nc, DN_HEADS, LANES), F32),
        ],
        scratch_shapes=[pltpu.VMEM((HALO + rt, 3 * D_MODEL), BF16)],
        compiler_params=pltpu.CompilerParams(
            dimension_semantics=("arbitrary", "arbitrary"), vmem_limit_bytes=VMEM_LIMIT),
        name="dn_prep",
    )(main, main, small, conv_w, vec, eg, eb, smat)


def _dn_solve_kernel(l_ref, t_ref, lt_ref, tt_ref):
    sub = 8
    nblk = CHUNK // sub
    lt_ref[...] = l_ref[...].T.reshape(CHUNK, CHUNK, LANES)
    tt_ref[...] = jnp.zeros_like(tt_ref)
    col = _iota((sub, LANES), 0)

    for ib in range(nblk):
        def row_body(i, carry, ib=ib):
            acc = [(col + k * sub == i).astype(F32) for k in range(ib + 1)]
            for jb in range(ib + 1):
                blk = lt_ref[i, jb * sub:(jb + 1) * sub, :]
                for jj in range(sub):
                    j = jb * sub + jj
                    lij = blk[jj:jj + 1, :]
                    for k in range(jb + 1):
                        acc[k] = acc[k] - lij * tt_ref[j, k * sub:(k + 1) * sub, :]
            tt_ref[i, 0:(ib + 1) * sub, :] = jnp.concatenate(acc, axis=0)
            return carry

        lax.fori_loop(ib * sub, (ib + 1) * sub, row_body, 0)

    t_ref[...] = tt_ref[...].reshape(CHUNK * CHUNK, LANES).T.astype(t_ref.dtype)


def _dn_solve(lmat):
    ni = lmat.shape[0]
    spec = pl.BlockSpec((LANES, CHUNK * CHUNK), lambda i: (i, 0))
    return pl.pallas_call(
        _dn_solve_kernel,
        grid=(pl.cdiv(ni, LANES),),
        in_specs=[spec],
        out_specs=spec,
        out_shape=jax.ShapeDtypeStruct(lmat.shape, BF16),
        scratch_shapes=[pltpu.VMEM((CHUNK, CHUNK, LANES), F32), pltpu.VMEM((CHUNK, CHUNK, LANES), F32)],
        compiler_params=pltpu.CompilerParams(
            dimension_semantics=("arbitrary",), vmem_limit_bytes=VMEM_LIMIT),
        name="dn_solve",
    )(lmat)


def _dn_scan_kernel(qt_ref, rhs_ref, kd_ref, t_ref, a_ref, egl_ref, z_ref, nw_ref, y_ref, s_ref,
                    *, chunks):
    @pl.when(pl.program_id(1) == 0)
    def _():
        s_ref[...] = jnp.zeros_like(s_ref)

    nw = nw_ref[...]
    heads = range(DN_HEADS)
    uw = [[_dot(t_ref[c, h], rhs_ref[c * CHUNK:(c + 1) * CHUNK, 2 * h * DN_HEAD_DIM:(2 * h + 2) * DN_HEAD_DIM])
           for h in heads] for c in range(chunks)]
    s = [s_ref[h] for h in heads]
    for c in range(chunks):
        rc = slice(c * CHUNK, (c + 1) * CHUNK)
        ws = []
        for h in heads:
            hs = slice(h * DN_HEAD_DIM, (h + 1) * DN_HEAD_DIM)
            wq = jnp.concatenate([uw[c][h][:, DN_HEAD_DIM:].astype(BF16), qt_ref[rc, hs]], axis=0)
            ws.append(_dot(wq, s[h].astype(BF16)))
        for h in heads:
            hs = slice(h * DN_HEAD_DIM, (h + 1) * DN_HEAD_DIM)
            vn = (uw[c][h][:, :DN_HEAD_DIM] - ws[h][:CHUNK]).astype(BF16)
            o = ws[h][CHUNK:] + _dot(a_ref[c, h], vn)
            s[h] = s[h] * egl_ref[c, h:h + 1, :] + _dot_tn(kd_ref[rc, hs], vn)
            ms = jnp.mean(o * o, axis=-1, keepdims=True)
            on = o * lax.rsqrt(ms + NORM_EPS) * nw
            y_ref[rc, hs] = (on * _silu(z_ref[rc, hs].astype(F32))).astype(BF16)
    for h in heads:
        s_ref[h] = s[h]


def _dn_scan(qt, rhs, kd, tmat, amat, egl, main, norm_w, batch, nc, chunks):
    rows = qt.shape[0]
    nt = nc // chunks
    rt = chunks * CHUNK
    tile, _, const, inst = _row_specs(rt, nt)
    return pl.pallas_call(
        functools.partial(_dn_scan_kernel, chunks=chunks),
        grid=(batch, nt),
        in_specs=[
            tile(D_MODEL, 0), tile(2 * D_MODEL, 0), tile(D_MODEL, 0), inst(CHUNK, CHUNK),
            inst(CHUNK, CHUNK), inst(LANES), tile(D_MODEL, COL_DN_Z), const((1, DN_HEAD_DIM)),
        ],
        out_specs=tile(D_MODEL, 0),
        out_shape=jax.ShapeDtypeStruct((rows, D_MODEL), BF16),
        scratch_shapes=[pltpu.VMEM((DN_HEADS, DN_HEAD_DIM, DN_HEAD_DIM), F32)],
        compiler_params=pltpu.CompilerParams(
            dimension_semantics=("arbitrary", "arbitrary"), vmem_limit_bytes=VMEM_LIMIT),
        name="dn_scan",
    )(qt, rhs, kd, tmat, amat, egl, main, norm_w)


def _ssd_kernel(xbc_ref, halo_ref, z_ref, small_ref, cw_ref, cb_ref, vec_ref, e16_ref, psel_ref,
                de_ref, nw_ref, smat_ref, y_ref, xe_ref, h_ref, *, chunks):
    t = pl.program_id(1)
    rows = chunks * CHUNK

    @pl.when(t == 0)
    def _():
        h_ref[...] = jnp.zeros_like(h_ref)

    _stage_rows(xbc_ref, halo_ref, xe_ref, t == 0)

    sm = small_ref[...]
    lane = _iota(sm.shape, 1)
    real = (t * rows + _iota(sm.shape, 0)) >= PAD_FRONT
    dt = jnp.where(real & (lane < SSD_HEADS), _softplus(sm + vec_ref[0:1, :]), 0.0)
    acum = _sel_dot(_chunk_tril_bf16(rows), dt * (-jnp.exp(vec_ref[1:2, :])))
    ae = _dot_sel(acum, e16_ref[...])
    dte = _dot_sel(dt, e16_ref[...], terms=2)
    alast = _last_row_of_chunks(ae, chunks)

    conv = []
    for j in range(SSD_XBC // CONV_COLS):
        cs = slice(j * CONV_COLS, (j + 1) * CONV_COLS)
        conv.append(_silu(_conv_cols(xe_ref, smat_ref, cw_ref, cs, rows) + cb_ref[:, cs]))
    nx = D_MODEL // CONV_COLS
    xs = jnp.concatenate(conv[:nx], axis=1)
    bc = jnp.concatenate(conv[nx:], axis=1).astype(BF16)
    xc = xs * dte
    xcb = xc.astype(BF16)
    xdec = (xc * jnp.exp(alast - ae)).astype(BF16)
    eae = jnp.exp(ae)
    ealast = [jnp.exp(ae[(c + 1) * CHUNK - 1:(c + 1) * CHUNK, :]) for c in range(chunks)]

    half = SSD_HEADS // SSD_GROUPS * SSD_HEAD_DIM
    pairs = SSD_HEADS // SSD_GROUPS // 2
    cr = [slice(c * CHUNK, (c + 1) * CHUNK) for c in range(chunks)]
    gsl = [slice(g * half, (g + 1) * half) for g in range(SSD_GROUPS)]
    bgs = [[bc[cr[c], g * SSD_STATE:(g + 1) * SSD_STATE] for g in range(SSD_GROUPS)] for c in range(chunks)]
    cgs = [[bc[cr[c], (SSD_GROUPS + g) * SSD_STATE:(SSD_GROUPS + g + 1) * SSD_STATE]
            for g in range(SSD_GROUPS)] for c in range(chunks)]

    states = [[_dot_tn(bgs[c][g], xdec[cr[c], gsl[g]]) for g in range(SSD_GROUPS)] for c in range(chunks)]
    yoff = [[None] * SSD_GROUPS for _ in range(chunks)]
    for g in range(SSD_GROUPS):
        hg = h_ref[g]
        for c in range(chunks):
            yoff[c][g] = _dot(cgs[c][g], hg.astype(BF16)) * eae[cr[c], gsl[g]]
            hg = hg * ealast[c][:, gsl[g]] + states[c][g]
        h_ref[g] = hg

    psel = psel_ref[...]
    prow = _iota((2 * CHUNK, LANES), 0)
    plane = _iota((2 * CHUNK, LANES), 1)
    keep = (prow < CHUNK) == ((plane & 1) == 0)
    bd_mask = (prow < CHUNK) == (plane < CHUNK)
    causal2 = _iota((CHUNK, LANES), 0) >= (_iota((CHUNK, LANES), 1) & (CHUNK - 1))
    cbcb = [[_dot_nt(cgs[c][g], jnp.concatenate([bgs[c][g], bgs[c][g]], axis=0))
             for g in range(SSD_GROUPS)] for c in range(chunks)]
    rowpair = []
    for c in range(chunks):
        ac = acum[cr[c]]
        a2 = jnp.where(keep, jnp.concatenate([ac, ac], axis=0), 0.0)
        rowpair.append(_sel_dot_nt(psel, a2))
    nw = nw_ref[...]
    de = de_ref[...]
    for c in range(chunks):
        ys = []
        for g in range(SSD_GROUPS):
            for p in range(pairs):
                j = g * pairs + p
                js = slice(j * LANES, (j + 1) * LANES)
                diff = ae[cr[c], js] - rowpair[c][j:j + 1, :]
                m = jnp.where(causal2, cbcb[c][g] * jnp.exp(jnp.minimum(diff, 0.0)), 0.0).astype(BF16)
                xp = xcb[cr[c], js]
                bd = jnp.where(bd_mask, jnp.concatenate([xp, xp], axis=0), jnp.zeros((), BF16))
                ys.append(_dot(m, bd) + yoff[c][g][:, p * LANES:(p + 1) * LANES])
        y = jnp.concatenate(ys, axis=1) + de * xs[cr[c]]
        y = y * _silu(z_ref[cr[c], :].astype(F32))
        for g in range(SSD_GROUPS):
            yg = y[:, gsl[g]]
            ms = jnp.mean(yg * yg, axis=-1, keepdims=True)
            y_ref[cr[c], gsl[g]] = (yg * lax.rsqrt(ms + NORM_EPS) * nw[:, gsl[g]]).astype(BF16)


def _ssd(main, small, conv_w, conv_b, vec, e16, psel, d_exp, norm_w, smat, batch, nc, chunks):
    rows = main.shape[0]
    nt = nc // chunks
    rt = chunks * CHUNK
    tile, halo, const, _ = _row_specs(rt, nt)
    return pl.pallas_call(
        functools.partial(_ssd_kernel, chunks=chunks),
        grid=(batch, nt),
        in_specs=[
            tile(SSD_XBC, COL_XBC), halo(SSD_XBC, COL_XBC), tile(D_MODEL, COL_SSD_Z), tile(LANES, 0),
            const((CONV_TAPS, SSD_XBC)), const((1, SSD_XBC)), const((8, LANES)),
            const((LANES, D_MODEL)), const((8, LANES)), const((1, D_MODEL)), const((1, D_MODEL)),
            const(smat.shape),
        ],
        out_specs=tile(D_MODEL, 0),
        out_shape=jax.ShapeDtypeStruct((rows, D_MODEL), BF16),
        scratch_shapes=[
            pltpu.VMEM((HALO + rt, SSD_XBC), BF16),
            pltpu.VMEM((SSD_GROUPS, SSD_STATE, D_MODEL // SSD_GROUPS), F32),
        ],
        compiler_params=pltpu.CompilerParams(
            dimension_semantics=("arbitrary", "arbitrary"), vmem_limit_bytes=VMEM_LIMIT),
        name="ssd",
    )(main, main, main, small, conv_w, conv_b, vec, e16, psel, d_exp, norm_w, smat)


def _out_kernel(yssd_ref, ydn_ref, b_ref, c_ref, chalo_ref, h_ref, hhalo_ref, g_ref, gates_ref,
                x_ref, scw_ref, wb_ref, wo_ref, np_ref, o_ref):
    t = pl.program_id(1)
    rows = x_ref.shape[0]
    sub = 8
    u = c_ref[...].astype(F32) * h_ref[...].astype(F32)
    uh = jnp.where(t == 0, 0.0, chalo_ref[...].astype(F32) * hhalo_ref[...].astype(F32))
    w0, w1, w2 = scw_ref[0:1, :], scw_ref[1:2, :], scw_ref[2:3, :]
    conv = w2 * u + w1 * pltpu.roll(u, 1, 0) + w0 * pltpu.roll(u, 2, 0)
    head = jnp.concatenate([uh, u[0:sub]], axis=0)
    fix = (w2 * head[HALO:HALO + sub] + w1 * head[HALO - 1:HALO - 1 + sub]
           + w0 * head[HALO - 2:HALO - 2 + sub])
    conv = jnp.concatenate([fix, conv[sub:]], axis=0)
    ysc = (b_ref[...].astype(F32) * conv * _silu(g_ref[...].astype(F32))).astype(BF16)

    merged = None
    for n, y in enumerate((yssd_ref[...], ysc, ydn_ref[...])):
        gate = _sigmoid(gates_ref[:, n * D_MODEL:(n + 1) * D_MODEL].astype(F32))
        term = gate * _dot(y, wb_ref[n])
        merged = term if merged is None else merged + term
    out = _dot(merged.astype(BF16), wo_ref[...])
    ms = jnp.mean(out * out, axis=-1, keepdims=True)
    res = x_ref[...] + out * lax.rsqrt(ms + NORM_EPS) * np_ref[...]
    real = (t * rows + _iota(res.shape, 0)) >= PAD_FRONT
    o_ref[...] = jnp.where(real, res, 0.0)


def _out(yssd, ydn, main, x, sc_w, w_branch, w_out, norm_post, batch, tp, rt):
    rows = x.shape[0]
    nt = tp // rt
    tile, halo, const, _ = _row_specs(rt, nt)
    return pl.pallas_call(
        _out_kernel,
        grid=(batch, nt),
        in_specs=[
            tile(D_MODEL, 0), tile(D_MODEL, 0),
            tile(D_MODEL, COL_SC_B), tile(D_MODEL, COL_SC_C), halo(D_MODEL, COL_SC_C),
            tile(D_MODEL, COL_SC_H), halo(D_MODEL, COL_SC_H), tile(D_MODEL, COL_SC_G),
            tile(3 * D_MODEL, COL_GATES), tile(D_MODEL, 0),
            const((3, D_MODEL)), const((3, D_MODEL, D_MODEL)), const((D_MODEL, D_MODEL)),
            const((1, D_MODEL)),
        ],
        out_specs=tile(D_MODEL, 0),
        out_shape=jax.ShapeDtypeStruct((rows, D_MODEL), F32),
        compiler_params=pltpu.CompilerParams(
            dimension_semantics=("arbitrary", "arbitrary"), vmem_limit_bytes=VMEM_LIMIT),
        name="out_proj",
    )(yssd, ydn, main, main, main, main, main, main, main, x, sc_w, w_branch, w_out, norm_post)


def _lane_vec(rows):
    out = jnp.zeros((8, LANES), F32)
    for i, (off, v) in enumerate(rows):
        out = out.at[i, off:off + v.shape[0]].set(v.astype(F32))
    return out


def _expand_matrix(lane0, heads, width):
    r = jnp.arange(LANES)[:, None]
    c = jnp.arange(heads * width)[None, :]
    return (r == lane0 + c // width).astype(BF16)


def _shift_matrix(rows):
    r = jnp.arange(CONV_TAPS * rows)[:, None]
    c = jnp.arange(HALO + rows)[None, :]
    return (c == HALO - (CONV_TAPS - 1) + r % rows + r // rows).astype(BF16)


def _split_w_in(w):
    sizes = (1024, 1536, 16, 1024, 1024, 1024, 1024, 3072, 1024, 8, 8, 3072)
    offs = [0]
    for s in sizes:
        offs.append(offs[-1] + s)
    (ssd_z, xbc, ssd_dt, sc_b, sc_c, sc_h, sc_g, qkv, dn_z, dn_b, dn_a, gates) = [
        w[:, offs[i]:offs[i + 1]] for i in range(len(sizes))]
    main = jnp.concatenate([qkv, gates, ssd_z, sc_g, dn_z, sc_b, sc_c, sc_h, xbc], axis=1).astype(BF16)
    small = jnp.concatenate(
        [ssd_dt, dn_b, dn_a, jnp.zeros((w.shape[0], LANES - 32), w.dtype)], axis=1).astype(BF16)
    return main, small


def kernel(x, meta_tokens, norm_pre, norm_post, w_in, ssd_conv_w, ssd_conv_b, ssd_dt_bias, ssd_a_log,
           ssd_d, ssd_norm, sc_conv_w, dn_conv_w, dn_dt_bias, dn_a_log, dn_norm, w_branch, w_out):
    batch, seq, d = x.shape
    assert d == D_MODEL and seq % CHUNK == 0
    tp = CHUNK + seq
    nc = tp // CHUNK
    chunks = 3 if nc % 3 == 0 else 1
    rows = batch * tp
    tm = tp // 2 if tp % 32 == 0 else tp
    rt_out = tp // 4 if tp % (4 * HALO) == 0 else tp
    depth = w_in.shape[0]

    meta = jnp.broadcast_to(meta_tokens.astype(x.dtype)[None], (batch, N_META, d))
    h = jnp.concatenate([jnp.zeros((batch, PAD_FRONT, d), x.dtype), meta, x], axis=1).reshape(rows, d)

    e16 = _expand_matrix(LANE_DT, SSD_HEADS, SSD_HEAD_DIM)
    eg = _expand_matrix(LANE_A, DN_HEADS, DN_HEAD_DIM)
    eb = _expand_matrix(LANE_BETA, DN_HEADS, DN_HEAD_DIM)
    smat = _shift_matrix(chunks * CHUNK)
    pj = jnp.arange(8)[:, None]
    pc = jnp.arange(LANES)[None, :]
    psel = ((pc // 2 == pj) & (pc < SSD_HEADS)).astype(BF16)

    for i in range(depth):
        w_main, w_small = _split_w_in(w_in[i])
        main, small = _in_proj(h, norm_pre[i][None, :], w_main, w_small, tm)

        dn_vec = _lane_vec([(LANE_A, dn_dt_bias[i]), (LANE_A, dn_a_log[i])])
        qt, rhs, kd, lmat, amat, egl = _dn_prep(
            main, small, dn_conv_w[i], dn_vec, eg, eb, smat, batch, nc, chunks)
        ni = batch * nc * DN_HEADS
        tmat = _dn_solve(lmat.reshape(ni, CHUNK * CHUNK)).reshape(batch * nc, DN_HEADS, CHUNK, CHUNK)
        ydn = _dn_scan(qt, rhs, kd, tmat, amat, egl, main, dn_norm[i][None, :], batch, nc, chunks)

        ssd_vec = _lane_vec([(LANE_DT, ssd_dt_bias[i]), (LANE_DT, ssd_a_log[i])])
        d_exp = jnp.repeat(ssd_d[i].astype(F32), SSD_HEAD_DIM)[None, :]
        yssd = _ssd(main, small, ssd_conv_w[i], ssd_conv_b[i][None, :], ssd_vec, e16, psel, d_exp,
                    ssd_norm[i][None, :], smat, batch, nc, chunks)

        h = _out(yssd, ydn, main, h, sc_conv_w[i], w_branch[i].astype(BF16), w_out[i].astype(BF16),
                 norm_post[i][None, :], batch, tp, rt_out)

    return h.reshape(batch, tp, d)[:, CHUNK:]
```

```python
import functools

import jax
import jax.numpy as jnp
from jax import lax
from jax.experimental import pallas as pl
from jax.experimental.pallas import tpu as pltpu

F32 = jnp.float32
BF16 = jnp.bfloat16

D_MODEL = 1024
N_META = 16
CHUNK = 64
CHUNK_SHIFT = 6
PAD_FRONT = CHUNK - N_META
NORM_EPS = 1e-6
SSD_HEADS = 16
SSD_HEAD_DIM = 64
SSD_GROUPS = 2
SSD_STATE = 128
SSD_XBC = 1536
DN_HEADS = 8
DN_HEAD_DIM = 128
CONV_TAPS = 4
LANES = 128
HALO = 16
CONV_COLS = 256
PROJ_TN = 2304
VMEM_LIMIT = 52 * 1024 * 1024

MAIN_WIDTH = 13824
COL_QKV = 0
COL_GATES = 1
COL_SSD_Z = 6
COL_SC_G = 7
COL_DN_Z = 8
COL_SC_B = 9
COL_SC_C = 10
COL_SC_H = 11
COL_XBC = 8
LANE_DT = 0
LANE_BETA = 16
LANE_A = 24


def _dot(a, b):
    return jnp.dot(a, b, preferred_element_type=F32)


def _dot_nt(a, b):
    return lax.dot_general(a, b, (((1,), (1,)), ((), ())), preferred_element_type=F32)


def _dot_tn(a, b):
    return lax.dot_general(a, b, (((0,), (0,)), ((), ())), preferred_element_type=F32)


def _split3(x):
    hi = x.astype(BF16)
    r1 = x - hi.astype(F32)
    mid = r1.astype(BF16)
    lo = (r1 - mid.astype(F32)).astype(BF16)
    return hi, mid, lo


def _dot_sel(x, sel, terms=3):
    parts = _split3(x)[:terms]
    out = _dot(parts[0], sel)
    for p in parts[1:]:
        out = out + _dot(p, sel)
    return out


def _sel_dot(sel, x):
    hi, mid, lo = _split3(x)
    return _dot(sel, hi) + _dot(sel, mid) + _dot(sel, lo)


def _sel_dot_nt(sel, x):
    hi, mid, lo = _split3(x)
    return _dot_nt(sel, hi) + _dot_nt(sel, mid) + _dot_nt(sel, lo)


def _sigmoid(x):
    return 1.0 / (1.0 + jnp.exp(-x))


def _silu(x):
    return x * _sigmoid(x)


def _softplus(x):
    return jnp.maximum(x, 0.0) + jnp.log(1.0 + jnp.exp(-jnp.abs(x)))


def _iota(shape, dim):
    return lax.broadcasted_iota(jnp.int32, shape, dim)


def _chunk_tril_bf16(rows):
    r = _iota((rows, rows), 0)
    c = _iota((rows, rows), 1)
    same = lax.shift_right_logical(r, CHUNK_SHIFT) == lax.shift_right_logical(c, CHUNK_SHIFT)
    return ((r >= c) & same).astype(BF16)


def _eye_bf16(n=LANES):
    return (_iota((n, n), 0) == _iota((n, n), 1)).astype(BF16)


def _last_row_of_chunks(x, chunks):
    return jnp.concatenate(
        [jnp.broadcast_to(x[(c + 1) * CHUNK - 1:(c + 1) * CHUNK, :], (CHUNK, x.shape[1]))
         for c in range(chunks)], axis=0)


def _stage_rows(x_ref, halo_ref, xe_ref, first):
    rows = x_ref.shape[0]
    halo = halo_ref[...]
    xe_ref[0:HALO, :] = jnp.where(first, jnp.zeros_like(halo), halo)
    xe_ref[HALO:HALO + rows, :] = x_ref[...]


def _conv_cols(xe_ref, smat_ref, w_ref, cs, rows):
    sh = _dot(smat_ref[...], xe_ref[:, cs])
    acc = None
    for k in range(CONV_TAPS):
        term = w_ref[k:k + 1, cs] * sh[k * rows:(k + 1) * rows]
        acc = term if acc is None else acc + term
    return acc


def _in_proj_kernel(x_ref, gain_ref, w_ref, ws_ref, main_ref, small_ref, xn_ref):
    @pl.when(pl.program_id(1) == 0)
    def _():
        x = x_ref[...]
        ms = jnp.mean(x * x, axis=-1, keepdims=True)
        xn = (x * lax.rsqrt(ms + NORM_EPS) * gain_ref[...]).astype(BF16)
        xn_ref[...] = xn
        small_ref[...] = _dot(xn, ws_ref[...])

    main_ref[...] = _dot(xn_ref[...], w_ref[...]).astype(BF16)


def _in_proj(h, gain, w_main, w_small, tm):
    rows = h.shape[0]
    grid = (rows // tm, MAIN_WIDTH // PROJ_TN)
    return pl.pallas_call(
        _in_proj_kernel,
        grid=grid,
        in_specs=[
            pl.BlockSpec((tm, D_MODEL), lambda i, j: (i, 0)),
            pl.BlockSpec((1, D_MODEL), lambda i, j: (0, 0)),
            pl.BlockSpec((D_MODEL, PROJ_TN), lambda i, j: (0, j)),
            pl.BlockSpec((D_MODEL, LANES), lambda i, j: (0, 0)),
        ],
        out_specs=[
            pl.BlockSpec((tm, PROJ_TN), lambda i, j: (i, j)),
            pl.BlockSpec((tm, LANES), lambda i, j: (i, 0)),
        ],
        out_shape=[
            jax.ShapeDtypeStruct((rows, MAIN_WIDTH), BF16),
            jax.ShapeDtypeStruct((rows, LANES), F32),
        ],
        scratch_shapes=[pltpu.VMEM((tm, D_MODEL), BF16)],
        compiler_params=pltpu.CompilerParams(
            dimension_semantics=("arbitrary", "arbitrary"), vmem_limit_bytes=VMEM_LIMIT),
        name="in_proj",
    )(h, gain, w_main, w_small)


def _row_specs(rt, nt):
    def tile(width, col):
        return pl.BlockSpec((rt, width), lambda b, t: (b * nt + t, col))

    def halo(width, col):
        return pl.BlockSpec(
            (HALO, width), lambda b, t: (jnp.maximum((b * nt + t) * (rt // HALO) - 1, 0), col))

    def const(shape):
        return pl.BlockSpec(shape, lambda b, t: (0,) * len(shape), pipeline_mode=pl.Buffered(1))

    def inst(*tail):
        shape = (rt // CHUNK, DN_HEADS) + tail
        return pl.BlockSpec(shape, lambda b, t: (b * nt + t,) + (0,) * (len(shape) - 1))

    return tile, halo, const, inst


def _dn_prep_kernel(qkv_ref, halo_ref, small_ref, cw_ref, vec_ref, eg_ref, eb_ref, smat_ref,
                    qt_ref, rhs_ref, kd_ref, l_ref, a_ref, egl_ref, xe_ref, *, chunks):
    t = pl.program_id(1)
    rows = chunks * CHUNK
    _stage_rows(qkv_ref, halo_ref, xe_ref, t == 0)

    sm = small_ref[...]
    lane = _iota(sm.shape, 1)
    real = (t * rows + _iota(sm.shape, 0)) >= PAD_FRONT
    is_beta = (lane >= LANE_BETA) & (lane < LANE_BETA + DN_HEADS)
    is_a = (lane >= LANE_A) & (lane < LANE_A + DN_HEADS)
    g = -jnp.exp(vec_ref[1:2, :]) * _softplus(sm + vec_ref[0:1, :])
    g = jnp.where(real & is_a, g, 0.0)
    beta = jnp.where(real & is_beta, _sigmoid(sm), 0.0)
    gcum = _sel_dot(_chunk_tril_bf16(rows), g)
    ge = _dot_sel(gcum, eg_ref[...])
    be = _dot_sel(beta, eb_ref[...], terms=2)
    gl = _last_row_of_chunks(ge, chunks)
    eg = jnp.exp(ge)
    ekd = jnp.exp(gl - ge)

    heads_per_block = CONV_COLS // DN_HEAD_DIM
    qkv = []
    for j in range(3 * D_MODEL // CONV_COLS):
        cs = slice(j * CONV_COLS, (j + 1) * CONV_COLS)
        val = _silu(_conv_cols(xe_ref, smat_ref, cw_ref, cs, rows))
        for i in range(heads_per_block):
            vh = val[:, i * DN_HEAD_DIM:(i + 1) * DN_HEAD_DIM]
            if j < 2 * D_MODEL // CONV_COLS:
                scale = DN_HEAD_DIM ** -0.5 if j < D_MODEL // CONV_COLS else 1.0
                ss = jnp.sum(vh * vh, axis=-1, keepdims=True)
                vh = vh * (lax.rsqrt(ss + NORM_EPS) * scale)
            qkv.append(vh)
    qs, ks, vs = qkv[:DN_HEADS], qkv[DN_HEADS:2 * DN_HEADS], qkv[2 * DN_HEADS:]

    for h in range(DN_HEADS):
        hs = slice(h * DN_HEAD_DIM, (h + 1) * DN_HEAD_DIM)
        bt = be[:, hs]
        qt_ref[:, hs] = (qs[h] * eg[:, hs]).astype(BF16)
        rhs_ref[:, 2 * h * DN_HEAD_DIM:(2 * h + 1) * DN_HEAD_DIM] = (vs[h] * bt).astype(BF16)
        rhs_ref[:, (2 * h + 1) * DN_HEAD_DIM:(2 * h + 2) * DN_HEAD_DIM] = (ks[h] * (bt * eg[:, hs])).astype(BF16)
        kd_ref[:, hs] = (ks[h] * ekd[:, hs]).astype(BF16)

    li = _iota((CHUNK, CHUNK), 0)
    si = _iota((CHUNK, CHUNK), 1)
    causal = li >= si
    strict = li > si
    eye = _eye_bf16()
    egl = [jnp.exp(ge[(c + 1) * CHUNK - 1:(c + 1) * CHUNK, :]) for c in range(chunks)]
    for c in range(chunks):
        rc = slice(c * CHUNK, (c + 1) * CHUNK)
        gct = _sel_dot_nt(eye, gcum[rc])
        for h in range(DN_HEADS):
            c0 = h * DN_HEAD_DIM
            kb = ks[h][rc].astype(BF16)
            kq = _dot_nt(jnp.concatenate([kb, qs[h][rc].astype(BF16)], axis=0), kb)
            gs = gct[LANE_A + h:LANE_A + h + 1, :]
            decay = jnp.where(causal, jnp.exp(jnp.minimum(ge[rc, c0:c0 + CHUNK] - gs, 0.0)), 0.0)
            l_ref[c, h] = jnp.where(strict, be[rc, c0:c0 + CHUNK] * kq[:CHUNK] * decay, 0.0)
            a_ref[c, h] = (kq[CHUNK:] * decay).astype(BF16)
            egl_ref[c, h:h + 1, :] = egl[c][:, c0:c0 + LANES]


def _dn_prep(main, small, conv_w, vec, eg, eb, smat, batch, nc, chunks):
    rows = main.shape[0]
    nt = nc // chunks
    rt = chunks * CHUNK
    tile, halo, const, inst = _row_specs(rt, nt)
    return pl.pallas_call(
        functools.partial(_dn_prep_kernel, chunks=chunks),
        grid=(batch, nt),
        in_specs=[
            tile(3 * D_MODEL, COL_QKV), halo(3 * D_MODEL, COL_QKV), tile(LANES, 0),
            const((CONV_TAPS, 3 * D_MODEL)), const((8, LANES)),
            const((LANES, D_MODEL)), const((LANES, D_MODEL)), const(smat.shape),
        ],
        out_specs=[
            tile(D_MODEL, 0), tile(2 * D_MODEL, 0), tile(D_MODEL, 0),
            inst(CHUNK, CHUNK), inst(CHUNK, CHUNK), inst(LANES),
        ],
        out_shape=[
            jax.ShapeDtypeStruct((rows, D_MODEL), BF16),
            jax.ShapeDtypeStruct((rows, 2 * D_MODEL), BF16),
            jax.ShapeDtypeStruct((rows, D_MODEL), BF16),
            jax.ShapeDtypeStruct((batch * nc, DN_HEADS, CHUNK, CHUNK), F32),
            jax.ShapeDtypeStruct((batch * nc, DN_HEADS, CHUNK, CHUNK), BF16),
            jax.ShapeDtypeStruct((batch * nc, DN_HEADS, LANES), F32),
        ],
        scratch_shapes=[pltpu.VMEM((HALO + rt, 3 * D_MODEL), BF16)],
        compiler_params=pltpu.CompilerParams(
            dimension_semantics=("arbitrary", "arbitrary"), vmem_limit_bytes=VMEM_LIMIT),
        name="dn_prep",
    )(main, main, small, conv_w, vec, eg, eb, smat)


def _dn_solve_kernel(lt_ref, tt_ref):
    sub = 8
    nblk = CHUNK // sub
    tt_ref[...] = jnp.zeros_like(tt_ref)
    col = _iota((sub, LANES), 0)

    for ib in range(nblk):
        def row_body(i, carry, ib=ib):
            acc = [(col + k * sub == i).astype(F32) for k in range(ib + 1)]
            for jb in range(ib + 1):
                blk = lt_ref[i, jb * sub:(jb + 1) * sub, :]
                for jj in range(sub):
                    j = jb * sub + jj
                    lij = blk[jj:jj + 1, :]
                    for k in range(jb + 1):
                        acc[k] = acc[k] - lij * tt_ref[j, k * sub:(k + 1) * sub, :]
            tt_ref[i, 0:(ib + 1) * sub, :] = jnp.concatenate(acc, axis=0)
            return carry

        lax.fori_loop(ib * sub, (ib + 1) * sub, row_body, 0)


def _dn_solve(lt):
    ni = lt.shape[-1]
    spec = pl.BlockSpec((CHUNK, CHUNK, LANES), lambda i: (0, 0, i))
    return pl.pallas_call(
        _dn_solve_kernel,
        grid=(pl.cdiv(ni, LANES),),
        in_specs=[spec],
        out_specs=spec,
        out_shape=jax.ShapeDtypeStruct(lt.shape, F32),
        compiler_params=pltpu.CompilerParams(dimension_semantics=("arbitrary",)),
        name="dn_solve",
    )(lt)


def _dn_scan_kernel(qt_ref, rhs_ref, kd_ref, t_ref, a_ref, egl_ref, z_ref, nw_ref, y_ref, s_ref,
                    *, chunks):
    @pl.when(pl.program_id(1) == 0)
    def _():
        s_ref[...] = jnp.zeros_like(s_ref)

    nw = nw_ref[...]
    heads = range(DN_HEADS)
    uw = [[_dot(t_ref[c, h], rhs_ref[c * CHUNK:(c + 1) * CHUNK, 2 * h * DN_HEAD_DIM:(2 * h + 2) * DN_HEAD_DIM])
           for h in heads] for c in range(chunks)]
    s = [s_ref[h] for h in heads]
    for c in range(chunks):
        rc = slice(c * CHUNK, (c + 1) * CHUNK)
        ws = []
        for h in heads:
            hs = slice(h * DN_HEAD_DIM, (h + 1) * DN_HEAD_DIM)
            wq = jnp.concatenate([uw[c][h][:, DN_HEAD_DIM:].astype(BF16), qt_ref[rc, hs]], axis=0)
            ws.append(_dot(wq, s[h].astype(BF16)))
        for h in heads:
            hs = slice(h * DN_HEAD_DIM, (h + 1) * DN_HEAD_DIM)
            vn = (uw[c][h][:, :DN_HEAD_DIM] - ws[h][:CHUNK]).astype(BF16)
            o = ws[h][CHUNK:] + _dot(a_ref[c, h], vn)
            s[h] = s[h] * egl_ref[c, h:h + 1, :] + _dot_tn(kd_ref[rc, hs], vn)
            ms = jnp.mean(o * o, axis=-1, keepdims=True)
            on = o * lax.rsqrt(ms + NORM_EPS) * nw
            y_ref[rc, hs] = (on * _silu(z_ref[rc, hs].astype(F32))).astype(BF16)
    for h in heads:
        s_ref[h] = s[h]


def _dn_scan(qt, rhs, kd, tmat, amat, egl, main, norm_w, batch, nc, chunks):
    rows = qt.shape[0]
    nt = nc // chunks
    rt = chunks * CHUNK
    tile, _, const, inst = _row_specs(rt, nt)
    return pl.pallas_call(
        functools.partial(_dn_scan_kernel, chunks=chunks),
        grid=(batch, nt),
        in_specs=[
            tile(D_MODEL, 0), tile(2 * D_MODEL, 0), tile(D_MODEL, 0), inst(CHUNK, CHUNK),
            inst(CHUNK, CHUNK), inst(LANES), tile(D_MODEL, COL_DN_Z), const((1, DN_HEAD_DIM)),
        ],
        out_specs=tile(D_MODEL, 0),
        out_shape=jax.ShapeDtypeStruct((rows, D_MODEL), BF16),
        scratch_shapes=[pltpu.VMEM((DN_HEADS, DN_HEAD_DIM, DN_HEAD_DIM), F32)],
        compiler_params=pltpu.CompilerParams(
            dimension_semantics=("arbitrary", "arbitrary"), vmem_limit_bytes=VMEM_LIMIT),
        name="dn_scan",
    )(qt, rhs, kd, tmat, amat, egl, main, norm_w)


def _ssd_kernel(xbc_ref, halo_ref, z_ref, small_ref, cw_ref, cb_ref, vec_ref, e16_ref, psel_ref,
                de_ref, nw_ref, smat_ref, y_ref, xe_ref, h_ref, *, chunks):
    t = pl.program_id(1)
    rows = chunks * CHUNK

    @pl.when(t == 0)
    def _():
        h_ref[...] = jnp.zeros_like(h_ref)

    _stage_rows(xbc_ref, halo_ref, xe_ref, t == 0)

    sm = small_ref[...]
    lane = _iota(sm.shape, 1)
    real = (t * rows + _iota(sm.shape, 0)) >= PAD_FRONT
    dt = jnp.where(real & (lane < SSD_HEADS), _softplus(sm + vec_ref[0:1, :]), 0.0)
    acum = _sel_dot(_chunk_tril_bf16(rows), dt * (-jnp.exp(vec_ref[1:2, :])))
    ae = _dot_sel(acum, e16_ref[...])
    dte = _dot_sel(dt, e16_ref[...], terms=2)
    alast = _last_row_of_chunks(ae, chunks)

    conv = []
    for j in range(SSD_XBC // CONV_COLS):
        cs = slice(j * CONV_COLS, (j + 1) * CONV_COLS)
        conv.append(_silu(_conv_cols(xe_ref, smat_ref, cw_ref, cs, rows) + cb_ref[:, cs]))
    nx = D_MODEL // CONV_COLS
    xs = jnp.concatenate(conv[:nx], axis=1)
    bc = jnp.concatenate(conv[nx:], axis=1).astype(BF16)
    xc = xs * dte
    xcb = xc.astype(BF16)
    xdec = (xc * jnp.exp(alast - ae)).astype(BF16)
    eae = jnp.exp(ae)
    ealast = [jnp.exp(ae[(c + 1) * CHUNK - 1:(c + 1) * CHUNK, :]) for c in range(chunks)]

    half = SSD_HEADS // SSD_GROUPS * SSD_HEAD_DIM
    pairs = SSD_HEADS // SSD_GROUPS // 2
    cr = [slice(c * CHUNK, (c + 1) * CHUNK) for c in range(chunks)]
    gsl = [slice(g * half, (g + 1) * half) for g in range(SSD_GROUPS)]
    bgs = [[bc[cr[c], g * SSD_STATE:(g + 1) * SSD_STATE] for g in range(SSD_GROUPS)] for c in range(chunks)]
    cgs = [[bc[cr[c], (SSD_GROUPS + g) * SSD_STATE:(SSD_GROUPS + g + 1) * SSD_STATE]
            for g in range(SSD_GROUPS)] for c in range(chunks)]

    states = [[_dot_tn(bgs[c][g], xdec[cr[c], gsl[g]]) for g in range(SSD_GROUPS)] for c in range(chunks)]
    yoff = [[None] * SSD_GROUPS for _ in range(chunks)]
    for g in range(SSD_GROUPS):
        hg = h_ref[g]
        for c in range(chunks):
            yoff[c][g] = _dot(cgs[c][g], hg.astype(BF16)) * eae[cr[c], gsl[g]]
            hg = hg * ealast[c][:, gsl[g]] + states[c][g]
        h_ref[g] = hg

    psel = psel_ref[...]
    prow = _iota((2 * CHUNK, LANES), 0)
    plane = _iota((2 * CHUNK, LANES), 1)
    keep = (prow < CHUNK) == ((plane & 1) == 0)
    bd_mask = (prow < CHUNK) == (plane < CHUNK)
    causal2 = _iota((CHUNK, LANES), 0) >= (_iota((CHUNK, LANES), 1) & (CHUNK - 1))
    cbcb = [[_dot_nt(cgs[c][g], jnp.concatenate([bgs[c][g], bgs[c][g]], axis=0))
             for g in range(SSD_GROUPS)] for c in range(chunks)]
    rowpair = []
    for c in range(chunks):
        ac = acum[cr[c]]
        a2 = jnp.where(keep, jnp.concatenate([ac, ac], axis=0), 0.0)
        rowpair.append(_sel_dot_nt(psel, a2))
    nw = nw_ref[...]
    de = de_ref[...]
    for c in range(chunks):
        ys = []
        for g in range(SSD_GROUPS):
            for p in range(pairs):
                j = g * pairs + p
                js = slice(j * LANES, (j + 1) * LANES)
                diff = ae[cr[c], js] - rowpair[c][j:j + 1, :]
                m = jnp.where(causal2, cbcb[c][g] * jnp.exp(jnp.minimum(diff, 0.0)), 0.0).astype(BF16)
                xp = xcb[cr[c], js]
                bd = jnp.where(bd_mask, jnp.concatenate([xp, xp], axis=0), jnp.zeros((), BF16))
                ys.append(_dot(m, bd) + yoff[c][g][:, p * LANES:(p + 1) * LANES])
        y = jnp.concatenate(ys, axis=1) + de * xs[cr[c]]
        y = y * _silu(z_ref[cr[c], :].astype(F32))
        for g in range(SSD_GROUPS):
            yg = y[:, gsl[g]]
            ms = jnp.mean(yg * yg, axis=-1, keepdims=True)
            y_ref[cr[c], gsl[g]] = (yg * lax.rsqrt(ms + NORM_EPS) * nw[:, gsl[g]]).astype(BF16)


def _ssd(main, small, conv_w, conv_b, vec, e16, psel, d_exp, norm_w, smat, batch, nc, chunks):
    rows = main.shape[0]
    nt = nc // chunks
    rt = chunks * CHUNK
    tile, halo, const, _ = _row_specs(rt, nt)
    return pl.pallas_call(
        functools.partial(_ssd_kernel, chunks=chunks),
        grid=(batch, nt),
        in_specs=[
            tile(SSD_XBC, COL_XBC), halo(SSD_XBC, COL_XBC), tile(D_MODEL, COL_SSD_Z), tile(LANES, 0),
            const((CONV_TAPS, SSD_XBC)), const((1, SSD_XBC)), const((8, LANES)),
            const((LANES, D_MODEL)), const((8, LANES)), const((1, D_MODEL)), const((1, D_MODEL)),
            const(smat.shape),
        ],
        out_specs=tile(D_MODEL, 0),
        out_shape=jax.ShapeDtypeStruct((rows, D_MODEL), BF16),
        scratch_shapes=[
            pltpu.VMEM((HALO + rt, SSD_XBC), BF16),
            pltpu.VMEM((SSD_GROUPS, SSD_STATE, D_MODEL // SSD_GROUPS), F32),
        ],
        compiler_params=pltpu.CompilerParams(
            dimension_semantics=("arbitrary", "arbitrary"), vmem_limit_bytes=VMEM_LIMIT),
        name="ssd",
    )(main, main, main, small, conv_w, conv_b, vec, e16, psel, d_exp, norm_w, smat)


def _out_kernel(yssd_ref, ydn_ref, b_ref, c_ref, chalo_ref, h_ref, hhalo_ref, g_ref, gates_ref,
                x_ref, scw_ref, wb_ref, wo_ref, np_ref, o_ref, *, row0):
    t = pl.program_id(1)
    rows = x_ref.shape[0]
    sub = 8
    u = c_ref[...].astype(F32) * h_ref[...].astype(F32)
    uh = chalo_ref[...].astype(F32) * hhalo_ref[...].astype(F32)
    if row0 == 0:
        uh = jnp.where(t == 0, 0.0, uh)
    w0, w1, w2 = scw_ref[0:1, :], scw_ref[1:2, :], scw_ref[2:3, :]
    conv = w2 * u + w1 * pltpu.roll(u, 1, 0) + w0 * pltpu.roll(u, 2, 0)
    head = jnp.concatenate([uh, u[0:sub]], axis=0)
    fix = (w2 * head[HALO:HALO + sub] + w1 * head[HALO - 1:HALO - 1 + sub]
           + w0 * head[HALO - 2:HALO - 2 + sub])
    conv = jnp.concatenate([fix, conv[sub:]], axis=0)
    ysc = (b_ref[...].astype(F32) * conv * _silu(g_ref[...].astype(F32))).astype(BF16)

    merged = None
    for n, y in enumerate((yssd_ref[...], ysc, ydn_ref[...])):
        gate = _sigmoid(gates_ref[:, n * D_MODEL:(n + 1) * D_MODEL].astype(F32))
        term = gate * _dot(y, wb_ref[n])
        merged = term if merged is None else merged + term
    out = _dot(merged.astype(BF16), wo_ref[...])
    ms = jnp.mean(out * out, axis=-1, keepdims=True)
    res = x_ref[...] + out * lax.rsqrt(ms + NORM_EPS) * np_ref[...]
    if row0 < PAD_FRONT:
        res = jnp.where((row0 + t * rows + _iota(res.shape, 0)) >= PAD_FRONT, res, 0.0)
    o_ref[...] = res


def _out(yssd, ydn, main, x, sc_w, w_branch, w_out, norm_post, batch, tp, rt, row0):
    nt = (tp - row0) // rt
    rows = batch * nt * rt
    _, _, const, _ = _row_specs(rt, nt)
    if row0 == 0:
        tile, halo, _, _ = _row_specs(rt, nt)
    else:
        def tile(width, col):
            return pl.BlockSpec(
                (pl.Element(rt), pl.Element(width)),
                lambda b, t: ((b * (tp // HALO) + row0 // HALO + t * (rt // HALO)) * HALO, col * width))

        def halo(width, col):
            return pl.BlockSpec(
                (pl.Element(HALO), pl.Element(width)),
                lambda b, t: ((b * (tp // HALO) + row0 // HALO + t * (rt // HALO) - 1) * HALO, col * width))

    out_spec = pl.BlockSpec((rt, D_MODEL), lambda b, t: (b * nt + t, 0))
    return pl.pallas_call(
        functools.partial(_out_kernel, row0=row0),
        grid=(batch, nt),
        in_specs=[
            tile(D_MODEL, 0), tile(D_MODEL, 0),
            tile(D_MODEL, COL_SC_B), tile(D_MODEL, COL_SC_C), halo(D_MODEL, COL_SC_C),
            tile(D_MODEL, COL_SC_H), halo(D_MODEL, COL_SC_H), tile(D_MODEL, COL_SC_G),
            tile(3 * D_MODEL, COL_GATES), tile(D_MODEL, 0),
            const((3, D_MODEL)), const((3, D_MODEL, D_MODEL)), const((D_MODEL, D_MODEL)),
            const((1, D_MODEL)),
        ],
        out_specs=out_spec,
        out_shape=jax.ShapeDtypeStruct((rows, D_MODEL), F32),
        compiler_params=pltpu.CompilerParams(
            dimension_semantics=("arbitrary", "arbitrary"), vmem_limit_bytes=VMEM_LIMIT),
        name="out_proj",
    )(yssd, ydn, main, main, main, main, main, main, main, x, sc_w, w_branch, w_out, norm_post)


def _lane_vec(rows):
    padded = [jnp.pad(v.astype(F32), (off, LANES - off - v.shape[0])) for off, v in rows]
    padded += [jnp.zeros((LANES,), F32)] * (8 - len(padded))
    return jnp.stack(padded)


def _expand_matrix(lane0, heads, width):
    r = jnp.arange(LANES)[:, None]
    c = jnp.arange(heads * width)[None, :]
    return (r == lane0 + c // width).astype(BF16)


def _shift_matrix(rows):
    r = jnp.arange(CONV_TAPS * rows)[:, None]
    c = jnp.arange(HALO + rows)[None, :]
    return (c == HALO - (CONV_TAPS - 1) + r % rows + r // rows).astype(BF16)


def _split_w_in(w):
    sizes = (1024, 1536, 16, 1024, 1024, 1024, 1024, 3072, 1024, 8, 8, 3072)
    offs = [0]
    for s in sizes:
        offs.append(offs[-1] + s)
    (ssd_z, xbc, ssd_dt, sc_b, sc_c, sc_h, sc_g, qkv, dn_z, dn_b, dn_a, gates) = [
        w[:, offs[i]:offs[i + 1]] for i in range(len(sizes))]
    main = jnp.concatenate([qkv, gates, ssd_z, sc_g, dn_z, sc_b, sc_c, sc_h, xbc], axis=1).astype(BF16)
    small = jnp.concatenate(
        [ssd_dt, dn_b, dn_a, jnp.zeros((w.shape[0], LANES - 32), w.dtype)], axis=1).astype(BF16)
    return main, small


def kernel(x, meta_tokens, norm_pre, norm_post, w_in, ssd_conv_w, ssd_conv_b, ssd_dt_bias, ssd_a_log,
           ssd_d, ssd_norm, sc_conv_w, dn_conv_w, dn_dt_bias, dn_a_log, dn_norm, w_branch, w_out):
    batch, seq, d = x.shape
    assert d == D_MODEL and seq % CHUNK == 0
    tp = CHUNK + seq
    nc = tp // CHUNK
    chunks = 3 if nc % 3 == 0 else 1
    rows = batch * tp
    tm = tp // 2 if tp % 32 == 0 else tp
    rt_out = tp // 4 if tp % (4 * HALO) == 0 else tp
    rt_last = seq // 4 if seq % (4 * CHUNK) == 0 else CHUNK
    depth = w_in.shape[0]

    meta = jnp.broadcast_to(meta_tokens.astype(x.dtype)[None], (batch, N_META, d))
    h = jnp.concatenate([jnp.zeros((batch, PAD_FRONT, d), x.dtype), meta, x], axis=1).reshape(rows, d)

    e16 = _expand_matrix(LANE_DT, SSD_HEADS, SSD_HEAD_DIM)
    eg = _expand_matrix(LANE_A, DN_HEADS, DN_HEAD_DIM)
    eb = _expand_matrix(LANE_BETA, DN_HEADS, DN_HEAD_DIM)
    smat = _shift_matrix(chunks * CHUNK)
    pj = jnp.arange(8)[:, None]
    pc = jnp.arange(LANES)[None, :]
    psel = ((pc // 2 == pj) & (pc < SSD_HEADS)).astype(BF16)

    for i in range(depth):
        w_main, w_small = _split_w_in(w_in[i])
        main, small = _in_proj(h, norm_pre[i][None, :], w_main, w_small, tm)

        dn_vec = _lane_vec([(LANE_A, dn_dt_bias[i]), (LANE_A, dn_a_log[i])])
        qt, rhs, kd, lmat, amat, egl = _dn_prep(
            main, small, dn_conv_w[i], dn_vec, eg, eb, smat, batch, nc, chunks)
        ni = batch * nc * DN_HEADS
        lt = jnp.transpose(lmat.reshape(ni, CHUNK, CHUNK), (1, 2, 0))
        if ni < LANES:
            lt = jnp.pad(lt, ((0, 0), (0, 0), (0, LANES - ni)))
        tmat = jnp.transpose(_dn_solve(lt)[:, :, :ni], (2, 0, 1)).astype(BF16).reshape(
            batch * nc, DN_HEADS, CHUNK, CHUNK)
        ydn = _dn_scan(qt, rhs, kd, tmat, amat, egl, main, dn_norm[i][None, :], batch, nc, chunks)

        ssd_vec = _lane_vec([(LANE_DT, ssd_dt_bias[i]), (LANE_DT, ssd_a_log[i])])
        d_exp = jnp.repeat(ssd_d[i].astype(F32), SSD_HEAD_DIM)[None, :]
        yssd = _ssd(main, small, ssd_conv_w[i], ssd_conv_b[i][None, :], ssd_vec, e16, psel, d_exp,
                    ssd_norm[i][None, :], smat, batch, nc, chunks)

        last = i == depth - 1
        h = _out(yssd, ydn, main, h, sc_conv_w[i], w_branch[i].astype(BF16), w_out[i].astype(BF16),
                 norm_post[i][None, :], batch, tp, rt_last if last else rt_out, CHUNK if last else 0)

    return h.reshape(batch, seq, d)
```

```python
import functools

import jax
import jax.numpy as jnp
from jax import lax
from jax.experimental import pallas as pl
from jax.experimental.pallas import tpu as pltpu

F32 = jnp.float32
BF16 = jnp.bfloat16

D_MODEL = 1024
N_META = 16
CHUNK = 64
CHUNK_SHIFT = 6
PAD_FRONT = CHUNK - N_META
NORM_EPS = 1e-6
SSD_HEADS = 16
SSD_HEAD_DIM = 64
SSD_GROUPS = 2
SSD_STATE = 128
SSD_XBC = 1536
DN_HEADS = 8
DN_HEAD_DIM = 128
CONV_TAPS = 4
LANES = 128
HALO = 16
CONV_COLS = 256
PROJ_TN = 2304
VMEM_LIMIT = 52 * 1024 * 1024

MAIN_WIDTH = 13824
COL_QKV = 0
COL_GATES = 1
COL_SSD_Z = 6
COL_SC_G = 7
COL_DN_Z = 8
COL_SC_B = 9
COL_SC_C = 10
COL_SC_H = 11
COL_XBC = 8
IN_SIZES = (1024, 1536, 16, 1024, 1024, 1024, 1024, 3072, 1024, 8, 8, 3072)
IN_OFFS = tuple(sum(IN_SIZES[:i]) for i in range(len(IN_SIZES) + 1))
OFF_DT, OFF_BETA, OFF_A = IN_OFFS[2], IN_OFFS[9], IN_OFFS[10]
LANE_DT = OFF_DT % LANES
LANE_BETA = OFF_BETA % LANES
LANE_A = OFF_A % LANES
assert OFF_BETA // LANES == OFF_A // LANES and LANE_DT + SSD_HEADS <= LANE_BETA < LANE_A


def _dot(a, b):
    return jnp.dot(a, b, preferred_element_type=F32)


def _dot_nt(a, b):
    return lax.dot_general(a, b, (((1,), (1,)), ((), ())), preferred_element_type=F32)


def _dot_tn(a, b):
    return lax.dot_general(a, b, (((0,), (0,)), ((), ())), preferred_element_type=F32)


def _split3(x):
    hi = x.astype(BF16)
    r1 = x - hi.astype(F32)
    mid = r1.astype(BF16)
    lo = (r1 - mid.astype(F32)).astype(BF16)
    return hi, mid, lo


def _dot_sel(x, sel, terms=3):
    parts = _split3(x)[:terms]
    out = _dot(parts[0], sel)
    for p in parts[1:]:
        out = out + _dot(p, sel)
    return out


def _sel_dot(sel, x):
    hi, mid, lo = _split3(x)
    return _dot(sel, hi) + _dot(sel, mid) + _dot(sel, lo)


def _sel_dot_nt(sel, x):
    hi, mid, lo = _split3(x)
    return _dot_nt(sel, hi) + _dot_nt(sel, mid) + _dot_nt(sel, lo)


def _sigmoid(x):
    return 1.0 / (1.0 + jnp.exp(-x))


def _silu(x):
    return x * _sigmoid(x)


def _softplus(x):
    return jnp.maximum(x, 0.0) + jnp.log(1.0 + jnp.exp(-jnp.abs(x)))


def _iota(shape, dim):
    return lax.broadcasted_iota(jnp.int32, shape, dim)


def _chunk_tril_bf16(rows):
    r = _iota((rows, rows), 0)
    c = _iota((rows, rows), 1)
    same = lax.shift_right_logical(r, CHUNK_SHIFT) == lax.shift_right_logical(c, CHUNK_SHIFT)
    return ((r >= c) & same).astype(BF16)


def _eye_bf16(n=LANES):
    return (_iota((n, n), 0) == _iota((n, n), 1)).astype(BF16)


def _last_row_of_chunks(x, chunks):
    return jnp.concatenate(
        [jnp.broadcast_to(x[(c + 1) * CHUNK - 1:(c + 1) * CHUNK, :], (CHUNK, x.shape[1]))
         for c in range(chunks)], axis=0)


def _stage_rows(x_ref, halo_ref, xe_ref, first):
    rows = x_ref.shape[0]
    halo = halo_ref[...]
    xe_ref[0:HALO, :] = jnp.where(first, jnp.zeros_like(halo), halo)
    xe_ref[HALO:HALO + rows, :] = x_ref[...]


def _conv_cols(xe_ref, smat_ref, w_ref, cs, rows):
    sh = _dot(smat_ref[...], xe_ref[:, cs])
    acc = w_ref[CONV_TAPS - 1:CONV_TAPS, cs] * xe_ref[HALO:HALO + rows, cs].astype(F32)
    for k in range(CONV_TAPS - 1):
        acc = acc + w_ref[k:k + 1, cs] * sh[k * rows:(k + 1) * rows]
    return acc


def _with_front_rows(x, front_ref, is_first):
    shifted = jnp.concatenate([front_ref[...], x[:x.shape[0] - CHUNK]], axis=0)
    return jnp.where(is_first, shifted, x)


def _unpadded_row_spec(rt, tiles_per_batch, seq):
    def index(b, t):
        row = b * (seq // HALO) + jnp.maximum(t * (rt // HALO) - CHUNK // HALO, 0)
        return row * HALO, 0

    return lambda grid_to_bt: pl.BlockSpec(
        (pl.Element(rt), pl.Element(D_MODEL)), lambda *g: index(*grid_to_bt(*g)))


def _in_proj_kernel(x_ref, front_ref, gain_ref, w_ref, wdt_ref, wba_ref, main_ref, small_ref, xn_ref,
                    *, unpadded, tiles_per_batch):
    @pl.when(pl.program_id(1) == 0)
    def _():
        x = x_ref[...]
        if unpadded:
            x = _with_front_rows(x, front_ref, pl.program_id(0) % tiles_per_batch == 0)
        ms = jnp.mean(x * x, axis=-1, keepdims=True)
        xn = (x * lax.rsqrt(ms + NORM_EPS) * gain_ref[...]).astype(BF16)
        xn_ref[...] = xn
        lane = _iota(wdt_ref.shape, 1)
        ws = jnp.where(lane < LANE_DT + SSD_HEADS, wdt_ref[...],
                       jnp.where((lane >= LANE_BETA) & (lane < LANE_A + DN_HEADS), wba_ref[...], 0.0))
        small_ref[...] = _dot(xn, ws.astype(BF16))

    main_ref[...] = _dot(xn_ref[...], w_ref[...]).astype(BF16)


def _in_proj(h, front, gain, w_main, w_orig, tm, tp, unpadded):
    tiles_per_batch = tp // tm
    rows = h.shape[0] // (tp - CHUNK) * tp if unpadded else h.shape[0]
    grid = (rows // tm, MAIN_WIDTH // PROJ_TN)
    if unpadded:
        x_spec = _unpadded_row_spec(tm, tiles_per_batch, tp - CHUNK)(
            lambda i, j: (i // tiles_per_batch, i % tiles_per_batch))
    else:
        x_spec = pl.BlockSpec((tm, D_MODEL), lambda i, j: (i, 0))
    return pl.pallas_call(
        functools.partial(_in_proj_kernel, unpadded=unpadded, tiles_per_batch=tiles_per_batch),
        grid=grid,
        in_specs=[
            x_spec,
            pl.BlockSpec((CHUNK, D_MODEL), lambda i, j: (0, 0)),
            pl.BlockSpec((1, D_MODEL), lambda i, j: (0, 0)),
            pl.BlockSpec((D_MODEL, PROJ_TN), lambda i, j: (0, j)),
            pl.BlockSpec((D_MODEL, LANES), lambda i, j: (0, OFF_DT // LANES)),
            pl.BlockSpec((D_MODEL, LANES), lambda i, j: (0, OFF_BETA // LANES)),
        ],
        out_specs=[
            pl.BlockSpec((tm, PROJ_TN), lambda i, j: (i, j)),
            pl.BlockSpec((tm, LANES), lambda i, j: (i, 0)),
        ],
        out_shape=[
            jax.ShapeDtypeStruct((rows, MAIN_WIDTH), BF16),
            jax.ShapeDtypeStruct((rows, LANES), F32),
        ],
        scratch_shapes=[pltpu.VMEM((tm, D_MODEL), BF16)],
        compiler_params=pltpu.CompilerParams(
            dimension_semantics=("arbitrary", "arbitrary"), vmem_limit_bytes=VMEM_LIMIT),
        name="in_proj",
    )(h, front, gain, w_main, w_orig, w_orig)


def _row_specs(rt, nt):
    def tile(width, col):
        return pl.BlockSpec((rt, width), lambda b, t: (b * nt + t, col))

    def halo(width, col):
        return pl.BlockSpec(
            (HALO, width), lambda b, t: (jnp.maximum((b * nt + t) * (rt // HALO) - 1, 0), col))

    def const(shape):
        return pl.BlockSpec(shape, lambda b, t: (0,) * len(shape), pipeline_mode=pl.Buffered(1))

    def inst(*tail):
        shape = (rt // CHUNK, DN_HEADS) + tail
        return pl.BlockSpec(shape, lambda b, t: (b * nt + t,) + (0,) * (len(shape) - 1))

    return tile, halo, const, inst


def _dn_prep_kernel(qkv_ref, halo_ref, small_ref, cw_ref, vec_ref, eg_ref, eb_ref, smat_ref,
                    qt_ref, rhs_ref, kd_ref, l_ref, a_ref, egl_ref, xe_ref, *, chunks):
    t = pl.program_id(1)
    rows = chunks * CHUNK
    _stage_rows(qkv_ref, halo_ref, xe_ref, t == 0)

    sm = small_ref[...]
    lane = _iota(sm.shape, 1)
    real = (t * rows + _iota(sm.shape, 0)) >= PAD_FRONT
    is_beta = (lane >= LANE_BETA) & (lane < LANE_BETA + DN_HEADS)
    is_a = (lane >= LANE_A) & (lane < LANE_A + DN_HEADS)
    g = -jnp.exp(vec_ref[1:2, :]) * _softplus(sm + vec_ref[0:1, :])
    g = jnp.where(real & is_a, g, 0.0)
    beta = jnp.where(real & is_beta, _sigmoid(sm), 0.0)
    gcum = _sel_dot(_chunk_tril_bf16(rows), g)
    ge = _dot_sel(gcum, eg_ref[...])
    be = _dot_sel(beta, eb_ref[...], terms=2)
    gl = _last_row_of_chunks(ge, chunks)
    eg = jnp.exp(ge)
    ekd = jnp.exp(gl - ge)

    li = _iota((CHUNK, CHUNK), 0)
    si = _iota((CHUNK, CHUNK), 1)
    causal = li >= si
    strict = li > si
    eye = _eye_bf16()
    cr = [slice(c * CHUNK, (c + 1) * CHUNK) for c in range(chunks)]
    egl = [jnp.exp(ge[cr[c].stop - 1:cr[c].stop, :]) for c in range(chunks)]
    gct = [_sel_dot_nt(eye, gcum[cr[c]]) for c in range(chunks)]

    def l2n(vh, scale):
        ss = jnp.sum(vh * vh, axis=-1, keepdims=True)
        return vh * (lax.rsqrt(ss + NORM_EPS) * scale)

    heads_per_block = CONV_COLS // DN_HEAD_DIM
    qkv = []
    for j in range(3 * D_MODEL // CONV_COLS):
        cs = slice(j * CONV_COLS, (j + 1) * CONV_COLS)
        val = _silu(_conv_cols(xe_ref, smat_ref, cw_ref, cs, rows))
        for i in range(heads_per_block):
            vh = val[:, i * DN_HEAD_DIM:(i + 1) * DN_HEAD_DIM]
            if j < D_MODEL // CONV_COLS:
                vh = l2n(vh, DN_HEAD_DIM ** -0.5)
            elif j < 2 * D_MODEL // CONV_COLS:
                vh = l2n(vh, 1.0)
            qkv.append(vh)
    qs, ks, vs = qkv[:DN_HEADS], qkv[DN_HEADS:2 * DN_HEADS], qkv[2 * DN_HEADS:]

    for h in range(DN_HEADS):
        hs = slice(h * DN_HEAD_DIM, (h + 1) * DN_HEAD_DIM)
        bt = be[:, hs]
        qt_ref[:, hs] = (qs[h] * eg[:, hs]).astype(BF16)
        rhs_ref[:, 2 * h * DN_HEAD_DIM:(2 * h + 1) * DN_HEAD_DIM] = (vs[h] * bt).astype(BF16)
        rhs_ref[:, (2 * h + 1) * DN_HEAD_DIM:(2 * h + 2) * DN_HEAD_DIM] = (ks[h] * (bt * eg[:, hs])).astype(BF16)
        kd_ref[:, hs] = (ks[h] * ekd[:, hs]).astype(BF16)

    for c in range(chunks):
        for h in range(DN_HEADS):
            c0 = h * DN_HEAD_DIM
            kb = ks[h][cr[c]].astype(BF16)
            kq = _dot_nt(jnp.concatenate([kb, qs[h][cr[c]].astype(BF16)], axis=0), kb)
            gs = gct[c][LANE_A + h:LANE_A + h + 1, :]
            decay = jnp.where(causal, jnp.exp(jnp.minimum(ge[cr[c], c0:c0 + CHUNK] - gs, 0.0)), 0.0)
            l_ref[c, h] = jnp.where(strict, be[cr[c], c0:c0 + CHUNK] * kq[:CHUNK] * decay, 0.0)
            a_ref[c, h] = (kq[CHUNK:] * decay).astype(BF16)
            egl_ref[c, h:h + 1, :] = egl[c][:, c0:c0 + LANES]


def _dn_prep(main, small, conv_w, vec, eg, eb, smat, batch, nc, chunks):
    rows = main.shape[0]
    nt = nc // chunks
    rt = chunks * CHUNK
    tile, halo, const, inst = _row_specs(rt, nt)
    return pl.pallas_call(
        functools.partial(_dn_prep_kernel, chunks=chunks),
        grid=(batch, nt),
        in_specs=[
            tile(3 * D_MODEL, COL_QKV), halo(3 * D_MODEL, COL_QKV), tile(LANES, 0),
            const((CONV_TAPS, 3 * D_MODEL)), const((8, LANES)),
            const((LANES, D_MODEL)), const((LANES, D_MODEL)), const(smat.shape),
        ],
        out_specs=[
            tile(D_MODEL, 0), tile(2 * D_MODEL, 0), tile(D_MODEL, 0),
            inst(CHUNK, CHUNK), inst(CHUNK, CHUNK), inst(LANES),
        ],
        out_shape=[
            jax.ShapeDtypeStruct((rows, D_MODEL), BF16),
            jax.ShapeDtypeStruct((rows, 2 * D_MODEL), BF16),
            jax.ShapeDtypeStruct((rows, D_MODEL), BF16),
            jax.ShapeDtypeStruct((batch * nc, DN_HEADS, CHUNK, CHUNK), F32),
            jax.ShapeDtypeStruct((batch * nc, DN_HEADS, CHUNK, CHUNK), BF16),
            jax.ShapeDtypeStruct((batch * nc, DN_HEADS, LANES), F32),
        ],
        scratch_shapes=[pltpu.VMEM((HALO + rt, 3 * D_MODEL), BF16)],
        compiler_params=pltpu.CompilerParams(
            dimension_semantics=("arbitrary", "arbitrary"), vmem_limit_bytes=VMEM_LIMIT),
        name="dn_prep",
    )(main, main, small, conv_w, vec, eg, eb, smat)


def _dn_solve_kernel(lt_ref, tt_ref):
    sub = 8
    nblk = CHUNK // sub
    tt_ref[...] = jnp.zeros_like(tt_ref)
    col = _iota((sub, LANES), 0)

    for ib in range(nblk):
        def row_body(i, carry, ib=ib):
            acc = [(col + k * sub == i).astype(F32) for k in range(ib + 1)]
            for jb in range(ib + 1):
                blk = lt_ref[i, jb * sub:(jb + 1) * sub, :]
                for jj in range(sub):
                    j = jb * sub + jj
                    lij = blk[jj:jj + 1, :]
                    for k in range(jb + 1):
                        acc[k] = acc[k] - lij * tt_ref[j, k * sub:(k + 1) * sub, :]
            tt_ref[i, 0:(ib + 1) * sub, :] = jnp.concatenate(acc, axis=0)
            return carry

        lax.fori_loop(ib * sub, (ib + 1) * sub, row_body, 0)


def _dn_solve(lt):
    ni = lt.shape[-1]
    spec = pl.BlockSpec((CHUNK, CHUNK, LANES), lambda i: (0, 0, i))
    return pl.pallas_call(
        _dn_solve_kernel,
        grid=(pl.cdiv(ni, LANES),),
        in_specs=[spec],
        out_specs=spec,
        out_shape=jax.ShapeDtypeStruct(lt.shape, F32),
        compiler_params=pltpu.CompilerParams(dimension_semantics=("arbitrary",)),
        name="dn_solve",
    )(lt)


def _dn_scan_kernel(qt_ref, rhs_ref, kd_ref, t_ref, a_ref, egl_ref, z_ref, nw_ref, y_ref, s_ref,
                    *, chunks):
    @pl.when(pl.program_id(1) == 0)
    def _():
        s_ref[...] = jnp.zeros_like(s_ref)

    nw = nw_ref[...]
    heads = range(DN_HEADS)
    uw = [[_dot(t_ref[c, h], rhs_ref[c * CHUNK:(c + 1) * CHUNK, 2 * h * DN_HEAD_DIM:(2 * h + 2) * DN_HEAD_DIM])
           for h in heads] for c in range(chunks)]
    s = [s_ref[h] for h in heads]
    for c in range(chunks):
        rc = slice(c * CHUNK, (c + 1) * CHUNK)
        ws = []
        for h in heads:
            hs = slice(h * DN_HEAD_DIM, (h + 1) * DN_HEAD_DIM)
            wq = jnp.concatenate([uw[c][h][:, DN_HEAD_DIM:].astype(BF16), qt_ref[rc, hs]], axis=0)
            ws.append(_dot(wq, s[h].astype(BF16)))
        for h in heads:
            hs = slice(h * DN_HEAD_DIM, (h + 1) * DN_HEAD_DIM)
            vn = (uw[c][h][:, :DN_HEAD_DIM] - ws[h][:CHUNK]).astype(BF16)
            o = ws[h][CHUNK:] + _dot(a_ref[c, h], vn)
            s[h] = s[h] * egl_ref[c, h:h + 1, :] + _dot_tn(kd_ref[rc, hs], vn)
            ms = jnp.mean(o * o, axis=-1, keepdims=True)
            on = o * lax.rsqrt(ms + NORM_EPS) * nw
            y_ref[rc, hs] = (on * _silu(z_ref[rc, hs].astype(F32))).astype(BF16)
    for h in heads:
        s_ref[h] = s[h]


def _dn_scan(qt, rhs, kd, tmat, amat, egl, main, norm_w, batch, nc, chunks):
    rows = qt.shape[0]
    nt = nc // chunks
    rt = chunks * CHUNK
    tile, _, const, inst = _row_specs(rt, nt)
    return pl.pallas_call(
        functools.partial(_dn_scan_kernel, chunks=chunks),
        grid=(batch, nt),
        in_specs=[
            tile(D_MODEL, 0), tile(2 * D_MODEL, 0), tile(D_MODEL, 0), inst(CHUNK, CHUNK),
            inst(CHUNK, CHUNK), inst(LANES), tile(D_MODEL, COL_DN_Z), const((1, DN_HEAD_DIM)),
        ],
        out_specs=tile(D_MODEL, 0),
        out_shape=jax.ShapeDtypeStruct((rows, D_MODEL), BF16),
        scratch_shapes=[pltpu.VMEM((DN_HEADS, DN_HEAD_DIM, DN_HEAD_DIM), F32)],
        compiler_params=pltpu.CompilerParams(
            dimension_semantics=("arbitrary", "arbitrary"), vmem_limit_bytes=VMEM_LIMIT),
        name="dn_scan",
    )(qt, rhs, kd, tmat, amat, egl, main, norm_w)


def _ssd_kernel(xbc_ref, halo_ref, z_ref, small_ref, cw_ref, cb_ref, vec_ref, e16_ref, psel_ref,
                de_ref, nw_ref, smat_ref, y_ref, xe_ref, h_ref, *, chunks):
    t = pl.program_id(1)
    rows = chunks * CHUNK

    @pl.when(t == 0)
    def _():
        h_ref[...] = jnp.zeros_like(h_ref)

    _stage_rows(xbc_ref, halo_ref, xe_ref, t == 0)

    sm = small_ref[...]
    lane = _iota(sm.shape, 1)
    real = (t * rows + _iota(sm.shape, 0)) >= PAD_FRONT
    dt = jnp.where(real & (lane < SSD_HEADS), _softplus(sm + vec_ref[0:1, :]), 0.0)
    acum = _sel_dot(_chunk_tril_bf16(rows), dt * (-jnp.exp(vec_ref[1:2, :])))
    ae = _dot_sel(acum, e16_ref[...])
    dte = _dot_sel(dt, e16_ref[...], terms=2)
    alast = _last_row_of_chunks(ae, chunks)

    conv = []
    for j in range(SSD_XBC // CONV_COLS):
        cs = slice(j * CONV_COLS, (j + 1) * CONV_COLS)
        conv.append(_silu(_conv_cols(xe_ref, smat_ref, cw_ref, cs, rows) + cb_ref[:, cs]))
    nx = D_MODEL // CONV_COLS
    xs = jnp.concatenate(conv[:nx], axis=1)
    bc = jnp.concatenate(conv[nx:], axis=1).astype(BF16)
    xc = xs * dte
    xcb = xc.astype(BF16)
    xdec = (xc * jnp.exp(alast - ae)).astype(BF16)
    eae = jnp.exp(ae)
    ealast = [jnp.exp(ae[(c + 1) * CHUNK - 1:(c + 1) * CHUNK, :]) for c in range(chunks)]

    half = SSD_HEADS // SSD_GROUPS * SSD_HEAD_DIM
    pairs = SSD_HEADS // SSD_GROUPS // 2
    cr = [slice(c * CHUNK, (c + 1) * CHUNK) for c in range(chunks)]
    gsl = [slice(g * half, (g + 1) * half) for g in range(SSD_GROUPS)]
    bgs = [[bc[cr[c], g * SSD_STATE:(g + 1) * SSD_STATE] for g in range(SSD_GROUPS)] for c in range(chunks)]
    cgs = [[bc[cr[c], (SSD_GROUPS + g) * SSD_STATE:(SSD_GROUPS + g + 1) * SSD_STATE]
            for g in range(SSD_GROUPS)] for c in range(chunks)]

    states = [[_dot_tn(bgs[c][g], xdec[cr[c], gsl[g]]) for g in range(SSD_GROUPS)] for c in range(chunks)]
    yoff = [[None] * SSD_GROUPS for _ in range(chunks)]
    for g in range(SSD_GROUPS):
        hg = h_ref[g]
        for c in range(chunks):
            yoff[c][g] = _dot(cgs[c][g], hg.astype(BF16)) * eae[cr[c], gsl[g]]
            hg = hg * ealast[c][:, gsl[g]] + states[c][g]
        h_ref[g] = hg

    psel = psel_ref[...]
    prow = _iota((2 * CHUNK, LANES), 0)
    plane = _iota((2 * CHUNK, LANES), 1)
    keep = (prow < CHUNK) == ((plane & 1) == 0)
    bd_mask = (prow < CHUNK) == (plane < CHUNK)
    causal2 = _iota((CHUNK, LANES), 0) >= (_iota((CHUNK, LANES), 1) & (CHUNK - 1))
    cbcb = [[_dot_nt(cgs[c][g], jnp.concatenate([bgs[c][g], bgs[c][g]], axis=0))
             for g in range(SSD_GROUPS)] for c in range(chunks)]
    rowpair = []
    for c in range(chunks):
        ac = acum[cr[c]]
        a2 = jnp.where(keep, jnp.concatenate([ac, ac], axis=0), 0.0)
        rowpair.append(_sel_dot_nt(psel, a2))
    nw = nw_ref[...]
    de = de_ref[...]
    for c in range(chunks):
        ys = []
        for g in range(SSD_GROUPS):
            for p in range(pairs):
                j = g * pairs + p
                js = slice(j * LANES, (j + 1) * LANES)
                diff = ae[cr[c], js] - rowpair[c][j:j + 1, :]
                m = jnp.where(causal2, cbcb[c][g] * jnp.exp(jnp.minimum(diff, 0.0)), 0.0).astype(BF16)
                xp = xcb[cr[c], js]
                bd = jnp.where(bd_mask, jnp.concatenate([xp, xp], axis=0), jnp.zeros((), BF16))
                ys.append(_dot(m, bd) + yoff[c][g][:, p * LANES:(p + 1) * LANES])
        y = jnp.concatenate(ys, axis=1) + de * xs[cr[c]]
        y = y * _silu(z_ref[cr[c], :].astype(F32))
        for g in range(SSD_GROUPS):
            yg = y[:, gsl[g]]
            ms = jnp.mean(yg * yg, axis=-1, keepdims=True)
            y_ref[cr[c], gsl[g]] = (yg * lax.rsqrt(ms + NORM_EPS) * nw[:, gsl[g]]).astype(BF16)


def _ssd(main, small, conv_w, conv_b, vec, e16, psel, d_exp, norm_w, smat, batch, nc, chunks):
    rows = main.shape[0]
    nt = nc // chunks
    rt = chunks * CHUNK
    tile, halo, const, _ = _row_specs(rt, nt)
    return pl.pallas_call(
        functools.partial(_ssd_kernel, chunks=chunks),
        grid=(batch, nt),
        in_specs=[
            tile(SSD_XBC, COL_XBC), halo(SSD_XBC, COL_XBC), tile(D_MODEL, COL_SSD_Z), tile(LANES, 0),
            const((CONV_TAPS, SSD_XBC)), const((1, SSD_XBC)), const((8, LANES)),
            const((LANES, D_MODEL)), const((8, LANES)), const((1, D_MODEL)), const((1, D_MODEL)),
            const(smat.shape),
        ],
        out_specs=tile(D_MODEL, 0),
        out_shape=jax.ShapeDtypeStruct((rows, D_MODEL), BF16),
        scratch_shapes=[
            pltpu.VMEM((HALO + rt, SSD_XBC), BF16),
            pltpu.VMEM((SSD_GROUPS, SSD_STATE, D_MODEL // SSD_GROUPS), F32),
        ],
        compiler_params=pltpu.CompilerParams(
            dimension_semantics=("arbitrary", "arbitrary"), vmem_limit_bytes=VMEM_LIMIT),
        name="ssd",
    )(main, main, main, small, conv_w, conv_b, vec, e16, psel, d_exp, norm_w, smat)


def _out_kernel(yssd_ref, ydn_ref, b_ref, c_ref, chalo_ref, h_ref, hhalo_ref, g_ref, gates_ref,
                x_ref, front_ref, scw_ref, wb_ref, wo_ref, np_ref, o_ref, *, row0, x_unpadded):
    t = pl.program_id(1)
    rows = x_ref.shape[0]
    sub = 8
    u = c_ref[...].astype(F32) * h_ref[...].astype(F32)
    uh = chalo_ref[...].astype(F32) * hhalo_ref[...].astype(F32)
    if row0 == 0:
        uh = jnp.where(t == 0, 0.0, uh)
    w0, w1, w2 = scw_ref[0:1, :], scw_ref[1:2, :], scw_ref[2:3, :]
    conv = w2 * u + w1 * pltpu.roll(u, 1, 0) + w0 * pltpu.roll(u, 2, 0)
    head = jnp.concatenate([uh, u[0:sub]], axis=0)
    fix = (w2 * head[HALO:HALO + sub] + w1 * head[HALO - 1:HALO - 1 + sub]
           + w0 * head[HALO - 2:HALO - 2 + sub])
    conv = jnp.concatenate([fix, conv[sub:]], axis=0)
    ysc = (b_ref[...].astype(F32) * conv * _silu(g_ref[...].astype(F32))).astype(BF16)

    merged = None
    for n, y in enumerate((yssd_ref[...], ysc, ydn_ref[...])):
        gate = _sigmoid(gates_ref[:, n * D_MODEL:(n + 1) * D_MODEL].astype(F32))
        term = gate * _dot(y, wb_ref[n])
        merged = term if merged is None else merged + term
    out = _dot(merged.astype(BF16), wo_ref[...])
    ms = jnp.mean(out * out, axis=-1, keepdims=True)
    resid = x_ref[...]
    if x_unpadded and row0 == 0:
        resid = _with_front_rows(resid, front_ref, t == 0)
    res = resid + out * lax.rsqrt(ms + NORM_EPS) * np_ref[...]
    if row0 < PAD_FRONT:
        res = jnp.where((row0 + t * rows + _iota(res.shape, 0)) >= PAD_FRONT, res, 0.0)
    o_ref[...] = res


def _out(yssd, ydn, main, x, front, sc_w, w_branch, w_out, norm_post, batch, tp, rt, row0, x_unpadded):
    nt = (tp - row0) // rt
    rows = batch * nt * rt
    _, _, const, _ = _row_specs(rt, nt)
    if row0 == 0:
        tile, halo, _, _ = _row_specs(rt, nt)
    else:
        def tile(width, col):
            return pl.BlockSpec(
                (pl.Element(rt), pl.Element(width)),
                lambda b, t: ((b * (tp // HALO) + row0 // HALO + t * (rt // HALO)) * HALO, col * width))

        def halo(width, col):
            return pl.BlockSpec(
                (pl.Element(HALO), pl.Element(width)),
                lambda b, t: ((b * (tp // HALO) + row0 // HALO + t * (rt // HALO) - 1) * HALO, col * width))

    out_spec = pl.BlockSpec((rt, D_MODEL), lambda b, t: (b * nt + t, 0))
    if not x_unpadded:
        x_spec = tile(D_MODEL, 0)
    elif row0 == 0:
        x_spec = _unpadded_row_spec(rt, nt, tp - CHUNK)(lambda b, t: (b, t))
    else:
        x_spec = out_spec
    return pl.pallas_call(
        functools.partial(_out_kernel, row0=row0, x_unpadded=x_unpadded),
        grid=(batch, nt),
        in_specs=[
            tile(D_MODEL, 0), tile(D_MODEL, 0),
            tile(D_MODEL, COL_SC_B), tile(D_MODEL, COL_SC_C), halo(D_MODEL, COL_SC_C),
            tile(D_MODEL, COL_SC_H), halo(D_MODEL, COL_SC_H), tile(D_MODEL, COL_SC_G),
            tile(3 * D_MODEL, COL_GATES), x_spec, const((CHUNK, D_MODEL)),
            const((3, D_MODEL)), const((3, D_MODEL, D_MODEL)), const((D_MODEL, D_MODEL)),
            const((1, D_MODEL)),
        ],
        out_specs=out_spec,
        out_shape=jax.ShapeDtypeStruct((rows, D_MODEL), F32),
        compiler_params=pltpu.CompilerParams(
            dimension_semantics=("arbitrary", "arbitrary"), vmem_limit_bytes=VMEM_LIMIT),
        name="out_proj",
    )(yssd, ydn, main, main, main, main, main, main, main, x, front, sc_w, w_branch, w_out, norm_post)


def _lane_vec(rows):
    padded = [jnp.pad(v.astype(F32), (off, LANES - off - v.shape[0])) for off, v in rows]
    padded += [jnp.zeros((LANES,), F32)] * (8 - len(padded))
    return jnp.stack(padded)


def _expand_matrix(lane0, heads, width):
    r = jnp.arange(LANES)[:, None]
    c = jnp.arange(heads * width)[None, :]
    return (r == lane0 + c // width).astype(BF16)


def _shift_matrix(rows):
    r = jnp.arange((CONV_TAPS - 1) * rows)[:, None]
    c = jnp.arange(HALO + rows)[None, :]
    return (c == HALO - (CONV_TAPS - 1) + r % rows + r // rows).astype(BF16)


def _split_w_in(w):
    (ssd_z, xbc, _, sc_b, sc_c, sc_h, sc_g, qkv, dn_z, _, _, gates) = [
        w[:, IN_OFFS[i]:IN_OFFS[i + 1]] for i in range(len(IN_SIZES))]
    return jnp.concatenate([qkv, gates, ssd_z, sc_g, dn_z, sc_b, sc_c, sc_h, xbc], axis=1).astype(BF16)


def kernel(x, meta_tokens, norm_pre, norm_post, w_in, ssd_conv_w, ssd_conv_b, ssd_dt_bias, ssd_a_log,
           ssd_d, ssd_norm, sc_conv_w, dn_conv_w, dn_dt_bias, dn_a_log, dn_norm, w_branch, w_out):
    batch, seq, d = x.shape
    assert d == D_MODEL and seq % CHUNK == 0
    tp = CHUNK + seq
    nc = tp // CHUNK
    chunks = 3 if nc % 3 == 0 else 1
    rows = batch * tp
    tm = tp // 2 if tp % 32 == 0 else tp
    rt_out = tp // 4 if tp % (4 * HALO) == 0 else tp
    rt_last = seq // 4 if seq % (4 * CHUNK) == 0 else CHUNK
    depth = w_in.shape[0]

    front = jnp.concatenate([jnp.zeros((PAD_FRONT, d), x.dtype), meta_tokens.astype(x.dtype)], axis=0)
    h = x.reshape(batch * seq, d)

    e16 = _expand_matrix(LANE_DT, SSD_HEADS, SSD_HEAD_DIM)
    eg = _expand_matrix(LANE_A, DN_HEADS, DN_HEAD_DIM)
    eb = _expand_matrix(LANE_BETA, DN_HEADS, DN_HEAD_DIM)
    smat = _shift_matrix(chunks * CHUNK)
    pj = jnp.arange(8)[:, None]
    pc = jnp.arange(LANES)[None, :]
    psel = ((pc // 2 == pj) & (pc < SSD_HEADS)).astype(BF16)

    for i in range(depth):
        unpadded = i == 0
        main, small = _in_proj(h, front, norm_pre[i][None, :], _split_w_in(w_in[i]), w_in[i], tm, tp, unpadded)

        dn_vec = _lane_vec([(LANE_A, dn_dt_bias[i]), (LANE_A, dn_a_log[i])])
        qt, rhs, kd, lmat, amat, egl = _dn_prep(
            main, small, dn_conv_w[i], dn_vec, eg, eb, smat, batch, nc, chunks)
        ni = batch * nc * DN_HEADS
        lt = jnp.transpose(lmat.reshape(ni, CHUNK, CHUNK), (1, 2, 0))
        if ni < LANES:
            lt = jnp.pad(lt, ((0, 0), (0, 0), (0, LANES - ni)))
        tmat = jnp.transpose(_dn_solve(lt)[:, :, :ni], (2, 0, 1)).astype(BF16).reshape(
            batch * nc, DN_HEADS, CHUNK, CHUNK)
        ydn = _dn_scan(qt, rhs, kd, tmat, amat, egl, main, dn_norm[i][None, :], batch, nc, chunks)

        ssd_vec = _lane_vec([(LANE_DT, ssd_dt_bias[i]), (LANE_DT, ssd_a_log[i])])
        d_exp = jnp.repeat(ssd_d[i].astype(F32), SSD_HEAD_DIM)[None, :]
        yssd = _ssd(main, small, ssd_conv_w[i], ssd_conv_b[i][None, :], ssd_vec, e16, psel, d_exp,
                    ssd_norm[i][None, :], smat, batch, nc, chunks)

        last = i == depth - 1
        h = _out(yssd, ydn, main, h, front, sc_conv_w[i], w_branch[i].astype(BF16), w_out[i].astype(BF16),
                 norm_post[i][None, :], batch, tp, rt_last if last else rt_out, CHUNK if last else 0,
                 unpadded)

    return h.reshape(batch, seq, d)
```

```python
import functools

import jax
import jax.numpy as jnp
from jax import lax
from jax.experimental import pallas as pl
from jax.experimental.pallas import tpu as pltpu

F32 = jnp.float32
BF16 = jnp.bfloat16

D_MODEL = 1024
N_META = 16
CHUNK = 64
CHUNK_SHIFT = 6
PAD_FRONT = CHUNK - N_META
NORM_EPS = 1e-6
SSD_HEADS = 16
SSD_HEAD_DIM = 64
SSD_GROUPS = 2
SSD_STATE = 128
SSD_XBC = 1536
DN_HEADS = 8
DN_HEAD_DIM = 128
CONV_TAPS = 4
LANES = 128
HALO = 16
CONV_COLS = 256
PROJ_TN = 2304
VMEM_LIMIT = 52 * 1024 * 1024

MAIN_WIDTH = 13824
COL_GATES = 0
COL_QKV = 1
COL_DN_Z = 6
COL_SC_B = 7
COL_SC_C = 8
COL_SC_H = 9
COL_SC_G = 10
COL_SSD_Z = 11
COL_XBC = 8
IN_SIZES = (1024, 1536, 16, 1024, 1024, 1024, 1024, 3072, 1024, 8, 8, 3072)
IN_OFFS = tuple(sum(IN_SIZES[:i]) for i in range(len(IN_SIZES) + 1))
OFF_DT, OFF_BETA, OFF_A = IN_OFFS[2], IN_OFFS[9], IN_OFFS[10]
LANE_DT = OFF_DT % LANES
LANE_BETA = OFF_BETA % LANES
LANE_A = OFF_A % LANES
assert OFF_BETA // LANES == OFF_A // LANES and LANE_DT + SSD_HEADS <= LANE_BETA < LANE_A


def _dot(a, b):
    return jnp.dot(a, b, preferred_element_type=F32)


def _dot_nt(a, b):
    return lax.dot_general(a, b, (((1,), (1,)), ((), ())), preferred_element_type=F32)


def _dot_tn(a, b):
    return lax.dot_general(a, b, (((0,), (0,)), ((), ())), preferred_element_type=F32)


def _split3(x):
    hi = x.astype(BF16)
    r1 = x - hi.astype(F32)
    mid = r1.astype(BF16)
    lo = (r1 - mid.astype(F32)).astype(BF16)
    return hi, mid, lo


def _dot_sel(x, sel, terms=3):
    parts = _split3(x)[:terms]
    out = _dot(parts[0], sel)
    for p in parts[1:]:
        out = out + _dot(p, sel)
    return out


def _sel_dot(sel, x):
    hi, mid, lo = _split3(x)
    return _dot(sel, hi) + _dot(sel, mid) + _dot(sel, lo)


def _sel_dot_nt(sel, x):
    hi, mid, lo = _split3(x)
    return _dot_nt(sel, hi) + _dot_nt(sel, mid) + _dot_nt(sel, lo)


def _sigmoid(x):
    return 1.0 / (1.0 + jnp.exp(-x))


def _silu(x):
    return x * _sigmoid(x)


def _softplus(x):
    return jnp.maximum(x, 0.0) + jnp.log(1.0 + jnp.exp(-jnp.abs(x)))


def _iota(shape, dim):
    return lax.broadcasted_iota(jnp.int32, shape, dim)


def _chunk_tril_bf16(rows):
    r = _iota((rows, rows), 0)
    c = _iota((rows, rows), 1)
    same = lax.shift_right_logical(r, CHUNK_SHIFT) == lax.shift_right_logical(c, CHUNK_SHIFT)
    return ((r >= c) & same).astype(BF16)


def _eye_bf16(n=LANES):
    return (_iota((n, n), 0) == _iota((n, n), 1)).astype(BF16)


def _last_row_of_chunks(x, chunks):
    return jnp.concatenate(
        [jnp.broadcast_to(x[(c + 1) * CHUNK - 1:(c + 1) * CHUNK, :], (CHUNK, x.shape[1]))
         for c in range(chunks)], axis=0)


def _stage_rows(x_ref, halo_ref, xe_ref, first):
    rows = x_ref.shape[0]
    halo = halo_ref[...]
    xe_ref[0:HALO, :] = jnp.where(first, jnp.zeros_like(halo), halo)
    xe_ref[HALO:HALO + rows, :] = x_ref[...]


def _conv_blocks(xe_ref, smat_ref, w_ref, width, rows):
    slices = [slice(j * CONV_COLS, (j + 1) * CONV_COLS) for j in range(width // CONV_COLS)]
    sh_next = _dot(smat_ref[...], xe_ref[:, slices[0]])
    for j, cs in enumerate(slices):
        sh = sh_next
        if j + 1 < len(slices):
            sh_next = _dot(smat_ref[...], xe_ref[:, slices[j + 1]])
        acc = w_ref[CONV_TAPS - 1:CONV_TAPS, cs] * xe_ref[HALO:HALO + rows, cs].astype(F32)
        for k in range(CONV_TAPS - 1):
            acc = acc + w_ref[k:k + 1, cs] * sh[k * rows:(k + 1) * rows]
        yield cs, acc


def _with_front_rows(x, front_ref, is_first):
    shifted = jnp.concatenate([front_ref[...], x[:x.shape[0] - CHUNK]], axis=0)
    return jnp.where(is_first, shifted, x)


def _unpadded_row_spec(rt, tiles_per_batch, seq):
    def index(b, t):
        row = b * (seq // HALO) + jnp.maximum(t * (rt // HALO) - CHUNK // HALO, 0)
        return row * HALO, 0

    return lambda grid_to_bt: pl.BlockSpec(
        (pl.Element(rt), pl.Element(D_MODEL)), lambda *g: index(*grid_to_bt(*g)))


def _in_proj_kernel(x_ref, front_ref, gain_ref, w_ref, wdt_ref, wba_ref, main_ref, small_ref, xn_ref,
                    *, unpadded, tiles_per_batch):
    @pl.when(pl.program_id(1) == 0)
    def _():
        x = x_ref[...]
        if unpadded:
            x = _with_front_rows(x, front_ref, pl.program_id(0) % tiles_per_batch == 0)
        ms = jnp.mean(x * x, axis=-1, keepdims=True)
        xn = (x * lax.rsqrt(ms + NORM_EPS) * gain_ref[...]).astype(BF16)
        xn_ref[...] = xn
        lane = _iota(wdt_ref.shape, 1)
        ws = jnp.where(lane < LANE_DT + SSD_HEADS, wdt_ref[...],
                       jnp.where((lane >= LANE_BETA) & (lane < LANE_A + DN_HEADS), wba_ref[...], 0.0))
        small_ref[...] = _dot(xn, ws.astype(BF16))

    main_ref[...] = _dot(xn_ref[...], w_ref[...]).astype(BF16)


def _in_proj(h, front, gain, w_main, w_dt_blk, w_ba_blk, tm, tp, unpadded):
    tiles_per_batch = tp // tm
    rows = h.shape[0] // (tp - CHUNK) * tp if unpadded else h.shape[0]
    grid = (rows // tm, MAIN_WIDTH // PROJ_TN)
    if unpadded:
        x_spec = _unpadded_row_spec(tm, tiles_per_batch, tp - CHUNK)(
            lambda i, j: (i // tiles_per_batch, i % tiles_per_batch))
    else:
        x_spec = pl.BlockSpec((tm, D_MODEL), lambda i, j: (i, 0))
    return pl.pallas_call(
        functools.partial(_in_proj_kernel, unpadded=unpadded, tiles_per_batch=tiles_per_batch),
        grid=grid,
        in_specs=[
            x_spec,
            pl.BlockSpec((CHUNK, D_MODEL), lambda i, j: (0, 0)),
            pl.BlockSpec((1, D_MODEL), lambda i, j: (0, 0)),
            pl.BlockSpec((D_MODEL, PROJ_TN), lambda i, j: (0, j)),
            pl.BlockSpec((D_MODEL, LANES), lambda i, j: (0, 0)),
            pl.BlockSpec((D_MODEL, LANES), lambda i, j: (0, 0)),
        ],
        out_specs=[
            pl.BlockSpec((tm, PROJ_TN), lambda i, j: (i, j)),
            pl.BlockSpec((tm, LANES), lambda i, j: (i, 0)),
        ],
        out_shape=[
            jax.ShapeDtypeStruct((rows, MAIN_WIDTH), BF16),
            jax.ShapeDtypeStruct((rows, LANES), F32),
        ],
        scratch_shapes=[pltpu.VMEM((tm, D_MODEL), BF16)],
        compiler_params=pltpu.CompilerParams(
            dimension_semantics=("arbitrary", "arbitrary"), vmem_limit_bytes=VMEM_LIMIT),
        name="in_proj",
    )(h, front, gain, w_main, w_dt_blk, w_ba_blk)


def _row_specs(rt, nt):
    def tile(width, col):
        return pl.BlockSpec((rt, width), lambda b, t: (b * nt + t, col))

    def halo(width, col):
        return pl.BlockSpec(
            (HALO, width), lambda b, t: (jnp.maximum((b * nt + t) * (rt // HALO) - 1, 0), col))

    def const(shape):
        return pl.BlockSpec(shape, lambda b, t: (0,) * len(shape), pipeline_mode=pl.Buffered(1))

    def inst(*tail):
        shape = (rt // CHUNK, DN_HEADS) + tail
        return pl.BlockSpec(shape, lambda b, t: (b * nt + t,) + (0,) * (len(shape) - 1))

    return tile, halo, const, inst


def _dn_prep_kernel(qkv_ref, halo_ref, small_ref, cw_ref, vec_ref, eg_ref, eb_ref, smat_ref,
                    qt_ref, rhs_ref, kd_ref, l_ref, a_ref, egl_ref, xe_ref, *, chunks):
    t = pl.program_id(1)
    rows = chunks * CHUNK
    _stage_rows(qkv_ref, halo_ref, xe_ref, t == 0)

    sm = small_ref[...]
    lane = _iota(sm.shape, 1)
    real = (t * rows + _iota(sm.shape, 0)) >= PAD_FRONT
    is_beta = (lane >= LANE_BETA) & (lane < LANE_BETA + DN_HEADS)
    is_a = (lane >= LANE_A) & (lane < LANE_A + DN_HEADS)
    g = -jnp.exp(vec_ref[1:2, :]) * _softplus(sm + vec_ref[0:1, :])
    g = jnp.where(real & is_a, g, 0.0)
    beta = jnp.where(real & is_beta, _sigmoid(sm), 0.0)
    gcum = _sel_dot(_chunk_tril_bf16(rows), g)
    ge = _dot_sel(gcum, eg_ref[...])
    be = _dot_sel(beta, eb_ref[...], terms=2)
    gl = _last_row_of_chunks(ge, chunks)
    eg = jnp.exp(ge)
    ekd = jnp.exp(gl - ge)

    li = _iota((CHUNK, CHUNK), 0)
    si = _iota((CHUNK, CHUNK), 1)
    causal = li >= si
    strict = li > si
    eye = _eye_bf16()
    cr = [slice(c * CHUNK, (c + 1) * CHUNK) for c in range(chunks)]
    egl = [jnp.exp(ge[cr[c].stop - 1:cr[c].stop, :]) for c in range(chunks)]
    gct = [_sel_dot_nt(eye, gcum[cr[c]]) for c in range(chunks)]

    def l2n(vh, scale):
        ss = jnp.sum(vh * vh, axis=-1, keepdims=True)
        return vh * (lax.rsqrt(ss + NORM_EPS) * scale)

    heads_per_block = CONV_COLS // DN_HEAD_DIM
    qkv = []
    for j, (cs, acc) in enumerate(_conv_blocks(xe_ref, smat_ref, cw_ref, 3 * D_MODEL, rows)):
        val = _silu(acc)
        for i in range(heads_per_block):
            vh = val[:, i * DN_HEAD_DIM:(i + 1) * DN_HEAD_DIM]
            if j < D_MODEL // CONV_COLS:
                vh = l2n(vh, DN_HEAD_DIM ** -0.5)
            elif j < 2 * D_MODEL // CONV_COLS:
                vh = l2n(vh, 1.0)
            qkv.append(vh)
    qs, ks, vs = qkv[:DN_HEADS], qkv[DN_HEADS:2 * DN_HEADS], qkv[2 * DN_HEADS:]

    for h in range(DN_HEADS):
        hs = slice(h * DN_HEAD_DIM, (h + 1) * DN_HEAD_DIM)
        bt = be[:, hs]
        qt_ref[:, hs] = (qs[h] * eg[:, hs]).astype(BF16)
        rhs_ref[:, 2 * h * DN_HEAD_DIM:(2 * h + 1) * DN_HEAD_DIM] = (vs[h] * bt).astype(BF16)
        rhs_ref[:, (2 * h + 1) * DN_HEAD_DIM:(2 * h + 2) * DN_HEAD_DIM] = (ks[h] * (bt * eg[:, hs])).astype(BF16)
        kd_ref[:, hs] = (ks[h] * ekd[:, hs]).astype(BF16)

    for c in range(chunks):
        for h in range(DN_HEADS):
            c0 = h * DN_HEAD_DIM
            kb = ks[h][cr[c]].astype(BF16)
            kq = _dot_nt(jnp.concatenate([kb, qs[h][cr[c]].astype(BF16)], axis=0), kb)
            gs = gct[c][LANE_A + h:LANE_A + h + 1, :]
            decay = jnp.where(causal, jnp.exp(jnp.minimum(ge[cr[c], c0:c0 + CHUNK] - gs, 0.0)), 0.0)
            l_ref[c, h] = jnp.where(strict, be[cr[c], c0:c0 + CHUNK] * kq[:CHUNK] * decay, 0.0)
            a_ref[c, h] = (kq[CHUNK:] * decay).astype(BF16)
            egl_ref[c, h:h + 1, :] = egl[c][:, c0:c0 + LANES]


def _dn_prep(main, small, conv_w, vec, eg, eb, smat, batch, nc, chunks):
    rows = main.shape[0]
    nt = nc // chunks
    rt = chunks * CHUNK
    tile, halo, const, inst = _row_specs(rt, nt)
    return pl.pallas_call(
        functools.partial(_dn_prep_kernel, chunks=chunks),
        grid=(batch, nt),
        in_specs=[
            tile(3 * D_MODEL, COL_QKV), halo(3 * D_MODEL, COL_QKV), tile(LANES, 0),
            const((CONV_TAPS, 3 * D_MODEL)), const((8, LANES)),
            const((LANES, D_MODEL)), const((LANES, D_MODEL)), const(smat.shape),
        ],
        out_specs=[
            tile(D_MODEL, 0), tile(2 * D_MODEL, 0), tile(D_MODEL, 0),
            inst(CHUNK, CHUNK), inst(CHUNK, CHUNK), inst(LANES),
        ],
        out_shape=[
            jax.ShapeDtypeStruct((rows, D_MODEL), BF16),
            jax.ShapeDtypeStruct((rows, 2 * D_MODEL), BF16),
            jax.ShapeDtypeStruct((rows, D_MODEL), BF16),
            jax.ShapeDtypeStruct((batch * nc, DN_HEADS, CHUNK, CHUNK), F32),
            jax.ShapeDtypeStruct((batch * nc, DN_HEADS, CHUNK, CHUNK), BF16),
            jax.ShapeDtypeStruct((batch * nc, DN_HEADS, LANES), F32),
        ],
        scratch_shapes=[pltpu.VMEM((HALO + rt, 3 * D_MODEL), BF16)],
        compiler_params=pltpu.CompilerParams(
            dimension_semantics=("arbitrary", "arbitrary"), vmem_limit_bytes=VMEM_LIMIT),
        name="dn_prep",
    )(main, main, small, conv_w, vec, eg, eb, smat)


def _dn_solve_kernel(lt_ref, tt_ref):
    sub = 8
    nblk = CHUNK // sub
    tt_ref[...] = jnp.zeros_like(tt_ref)
    col = _iota((sub, LANES), 0)

    for ib in range(nblk):
        def row_body(i, carry, ib=ib):
            acc = [(col + k * sub == i).astype(F32) for k in range(ib + 1)]
            for jb in range(ib + 1):
                blk = lt_ref[i, jb * sub:(jb + 1) * sub, :]
                for jj in range(sub):
                    j = jb * sub + jj
                    lij = blk[jj:jj + 1, :]
                    for k in range(jb + 1):
                        acc[k] = acc[k] - lij * tt_ref[j, k * sub:(k + 1) * sub, :]
            tt_ref[i, 0:(ib + 1) * sub, :] = jnp.concatenate(acc, axis=0)
            return carry

        lax.fori_loop(ib * sub, (ib + 1) * sub, row_body, 0)


def _dn_solve(lt):
    ni = lt.shape[-1]
    spec = pl.BlockSpec((CHUNK, CHUNK, LANES), lambda i: (0, 0, i))
    return pl.pallas_call(
        _dn_solve_kernel,
        grid=(pl.cdiv(ni, LANES),),
        in_specs=[spec],
        out_specs=spec,
        out_shape=jax.ShapeDtypeStruct(lt.shape, F32),
        compiler_params=pltpu.CompilerParams(dimension_semantics=("arbitrary",)),
        name="dn_solve",
    )(lt)


def _dn_scan_kernel(qt_ref, rhs_ref, kd_ref, t_ref, a_ref, egl_ref, z_ref, nw_ref, y_ref, s_ref,
                    *, chunks, nb):
    @pl.when(pl.program_id(1) == 0)
    def _():
        s_ref[...] = jnp.zeros_like(s_ref)

    nw = nw_ref[...]
    units = [(b, h) for b in range(nb) for h in range(DN_HEADS)]
    uw = [[_dot(t_ref[b, c, h],
                rhs_ref[b, c * CHUNK:(c + 1) * CHUNK, 2 * h * DN_HEAD_DIM:(2 * h + 2) * DN_HEAD_DIM])
           for (b, h) in units] for c in range(chunks)]
    s = [s_ref[b * DN_HEADS + h] for (b, h) in units]
    for c in range(chunks):
        rc = slice(c * CHUNK, (c + 1) * CHUNK)
        ws = []
        for i, (b, h) in enumerate(units):
            hs = slice(h * DN_HEAD_DIM, (h + 1) * DN_HEAD_DIM)
            wq = jnp.concatenate([uw[c][i][:, DN_HEAD_DIM:].astype(BF16), qt_ref[b, rc, hs]], axis=0)
            ws.append(_dot(wq, s[i].astype(BF16)))
        for i, (b, h) in enumerate(units):
            hs = slice(h * DN_HEAD_DIM, (h + 1) * DN_HEAD_DIM)
            vn = (uw[c][i][:, :DN_HEAD_DIM] - ws[i][:CHUNK]).astype(BF16)
            o = ws[i][CHUNK:] + _dot(a_ref[b, c, h], vn)
            s[i] = s[i] * egl_ref[b, c, h:h + 1, :] + _dot_tn(kd_ref[b, rc, hs], vn)
            ms = jnp.mean(o * o, axis=-1, keepdims=True)
            on = o * lax.rsqrt(ms + NORM_EPS) * nw
            y_ref[b, rc, hs] = (on * _silu(z_ref[b, rc, hs].astype(F32))).astype(BF16)
    for i, (b, h) in enumerate(units):
        s_ref[b * DN_HEADS + h] = s[i]


def _dn_scan(qt, rhs, kd, tmat, amat, egl, main, norm_w, batch, nc, chunks):
    tp = nc * CHUNK
    nt = nc // chunks
    rt = chunks * CHUNK
    nb = 2 if batch % 2 == 0 else 1

    def seq_tile(width, col):
        return pl.BlockSpec((nb, rt, width), lambda p, t: (p, t, col))

    def inst(*tail):
        shape = (nb, chunks, DN_HEADS) + tail
        return pl.BlockSpec(shape, lambda p, t: (p, t) + (0,) * (len(shape) - 2))

    def by_seq(a):
        return a.reshape((batch, a.shape[0] // batch) + a.shape[1:])

    y = pl.pallas_call(
        functools.partial(_dn_scan_kernel, chunks=chunks, nb=nb),
        grid=(batch // nb, nt),
        in_specs=[
            seq_tile(D_MODEL, 0), seq_tile(2 * D_MODEL, 0), seq_tile(D_MODEL, 0), inst(CHUNK, CHUNK),
            inst(CHUNK, CHUNK), inst(LANES), seq_tile(D_MODEL, COL_DN_Z),
            pl.BlockSpec((1, DN_HEAD_DIM), lambda p, t: (0, 0), pipeline_mode=pl.Buffered(1)),
        ],
        out_specs=seq_tile(D_MODEL, 0),
        out_shape=jax.ShapeDtypeStruct((batch, tp, D_MODEL), BF16),
        scratch_shapes=[pltpu.VMEM((nb * DN_HEADS, DN_HEAD_DIM, DN_HEAD_DIM), F32)],
        compiler_params=pltpu.CompilerParams(
            dimension_semantics=("arbitrary", "arbitrary"), vmem_limit_bytes=VMEM_LIMIT),
        name="dn_scan",
    )(by_seq(qt), by_seq(rhs), by_seq(kd), by_seq(tmat), by_seq(amat), by_seq(egl), by_seq(main), norm_w)
    return y.reshape(batch * tp, D_MODEL)


def _ssd_kernel(xbc_ref, halo_ref, z_ref, small_ref, cw_ref, cb_ref, vec_ref, e16_ref, psel_ref,
                de_ref, nw_ref, smat_ref, y_ref, xe_ref, h_ref, *, chunks):
    t = pl.program_id(1)
    rows = chunks * CHUNK

    @pl.when(t == 0)
    def _():
        h_ref[...] = jnp.zeros_like(h_ref)

    _stage_rows(xbc_ref, halo_ref, xe_ref, t == 0)

    sm = small_ref[...]
    lane = _iota(sm.shape, 1)
    real = (t * rows + _iota(sm.shape, 0)) >= PAD_FRONT
    dt = jnp.where(real & (lane < SSD_HEADS), _softplus(sm + vec_ref[0:1, :]), 0.0)
    acum = _sel_dot(_chunk_tril_bf16(rows), dt * (-jnp.exp(vec_ref[1:2, :])))
    ae = _dot_sel(acum, e16_ref[...])
    dte = _dot_sel(dt, e16_ref[...], terms=2)
    alast = _last_row_of_chunks(ae, chunks)

    conv = [_silu(acc + cb_ref[:, cs]) for cs, acc in _conv_blocks(xe_ref, smat_ref, cw_ref, SSD_XBC, rows)]
    nx = D_MODEL // CONV_COLS
    xs = jnp.concatenate(conv[:nx], axis=1)
    bc = jnp.concatenate(conv[nx:], axis=1).astype(BF16)
    xc = xs * dte
    xcb = xc.astype(BF16)
    xdec = (xc * jnp.exp(alast - ae)).astype(BF16)
    eae = jnp.exp(ae)
    ealast = [jnp.exp(ae[(c + 1) * CHUNK - 1:(c + 1) * CHUNK, :]) for c in range(chunks)]

    half = SSD_HEADS // SSD_GROUPS * SSD_HEAD_DIM
    pairs = SSD_HEADS // SSD_GROUPS // 2
    cr = [slice(c * CHUNK, (c + 1) * CHUNK) for c in range(chunks)]
    gsl = [slice(g * half, (g + 1) * half) for g in range(SSD_GROUPS)]
    bgs = [[bc[cr[c], g * SSD_STATE:(g + 1) * SSD_STATE] for g in range(SSD_GROUPS)] for c in range(chunks)]
    cgs = [[bc[cr[c], (SSD_GROUPS + g) * SSD_STATE:(SSD_GROUPS + g + 1) * SSD_STATE]
            for g in range(SSD_GROUPS)] for c in range(chunks)]

    states = [[_dot_tn(bgs[c][g], xdec[cr[c], gsl[g]]) for g in range(SSD_GROUPS)] for c in range(chunks)]
    yoff = [[None] * SSD_GROUPS for _ in range(chunks)]
    for g in range(SSD_GROUPS):
        hg = h_ref[g]
        for c in range(chunks):
            yoff[c][g] = _dot(cgs[c][g], hg.astype(BF16)) * eae[cr[c], gsl[g]]
            hg = hg * ealast[c][:, gsl[g]] + states[c][g]
        h_ref[g] = hg

    psel = psel_ref[...]
    prow = _iota((2 * CHUNK, LANES), 0)
    plane = _iota((2 * CHUNK, LANES), 1)
    keep = (prow < CHUNK) == ((plane & 1) == 0)
    bd_mask = (prow < CHUNK) == (plane < CHUNK)
    causal2 = _iota((CHUNK, LANES), 0) >= (_iota((CHUNK, LANES), 1) & (CHUNK - 1))
    cbcb = [[_dot_nt(cgs[c][g], jnp.concatenate([bgs[c][g], bgs[c][g]], axis=0))
             for g in range(SSD_GROUPS)] for c in range(chunks)]
    rowpair = []
    for c in range(chunks):
        ac = acum[cr[c]]
        a2 = jnp.where(keep, jnp.concatenate([ac, ac], axis=0), 0.0)
        rowpair.append(_sel_dot_nt(psel, a2))
    nw = nw_ref[...]
    de = de_ref[...]
    for c in range(chunks):
        ys = []
        for g in range(SSD_GROUPS):
            for p in range(pairs):
                j = g * pairs + p
                js = slice(j * LANES, (j + 1) * LANES)
                diff = ae[cr[c], js] - rowpair[c][j:j + 1, :]
                m = jnp.where(causal2, cbcb[c][g] * jnp.exp(jnp.minimum(diff, 0.0)), 0.0).astype(BF16)
                xp = xcb[cr[c], js]
                bd = jnp.where(bd_mask, jnp.concatenate([xp, xp], axis=0), jnp.zeros((), BF16))
                ys.append(_dot(m, bd) + yoff[c][g][:, p * LANES:(p + 1) * LANES])
        y = jnp.concatenate(ys, axis=1) + de * xs[cr[c]]
        y = y * _silu(z_ref[cr[c], :].astype(F32))
        for g in range(SSD_GROUPS):
            yg = y[:, gsl[g]]
            ms = jnp.mean(yg * yg, axis=-1, keepdims=True)
            y_ref[cr[c], gsl[g]] = (yg * lax.rsqrt(ms + NORM_EPS) * nw[:, gsl[g]]).astype(BF16)


def _ssd(main, small, conv_w, conv_b, vec, e16, psel, d_exp, norm_w, smat, batch, nc, chunks):
    rows = main.shape[0]
    nt = nc // chunks
    rt = chunks * CHUNK
    tile, halo, const, _ = _row_specs(rt, nt)
    return pl.pallas_call(
        functools.partial(_ssd_kernel, chunks=chunks),
        grid=(batch, nt),
        in_specs=[
            tile(SSD_XBC, COL_XBC), halo(SSD_XBC, COL_XBC), tile(D_MODEL, COL_SSD_Z), tile(LANES, 0),
            const((CONV_TAPS, SSD_XBC)), const((1, SSD_XBC)), const((8, LANES)),
            const((LANES, D_MODEL)), const((8, LANES)), const((1, D_MODEL)), const((1, D_MODEL)),
            const(smat.shape),
        ],
        out_specs=tile(D_MODEL, 0),
        out_shape=jax.ShapeDtypeStruct((rows, D_MODEL), BF16),
        scratch_shapes=[
            pltpu.VMEM((HALO + rt, SSD_XBC), BF16),
            pltpu.VMEM((SSD_GROUPS, SSD_STATE, D_MODEL // SSD_GROUPS), F32),
        ],
        compiler_params=pltpu.CompilerParams(
            dimension_semantics=("arbitrary", "arbitrary"), vmem_limit_bytes=VMEM_LIMIT),
        name="ssd",
    )(main, main, main, small, conv_w, conv_b, vec, e16, psel, d_exp, norm_w, smat)


def _out_kernel(yssd_ref, ydn_ref, b_ref, c_ref, chalo_ref, h_ref, hhalo_ref, g_ref, gates_ref,
                x_ref, front_ref, scw_ref, wb_ref, wo_ref, np_ref, o_ref, *, row0, x_unpadded):
    t = pl.program_id(1)
    rows = x_ref.shape[0]
    sub = 8
    u = c_ref[...].astype(F32) * h_ref[...].astype(F32)
    uh = chalo_ref[...].astype(F32) * hhalo_ref[...].astype(F32)
    if row0 == 0:
        uh = jnp.where(t == 0, 0.0, uh)
    w0, w1, w2 = scw_ref[0:1, :], scw_ref[1:2, :], scw_ref[2:3, :]
    conv = w2 * u + w1 * pltpu.roll(u, 1, 0) + w0 * pltpu.roll(u, 2, 0)
    head = jnp.concatenate([uh, u[0:sub]], axis=0)
    fix = (w2 * head[HALO:HALO + sub] + w1 * head[HALO - 1:HALO - 1 + sub]
           + w0 * head[HALO - 2:HALO - 2 + sub])
    conv = jnp.concatenate([fix, conv[sub:]], axis=0)
    ysc = (b_ref[...].astype(F32) * conv * _silu(g_ref[...].astype(F32))).astype(BF16)

    merged = None
    for n, y in enumerate((yssd_ref[...], ysc, ydn_ref[...])):
        gate = _sigmoid(gates_ref[:, n * D_MODEL:(n + 1) * D_MODEL].astype(F32))
        term = gate * _dot(y, wb_ref[n])
        merged = term if merged is None else merged + term
    out = _dot(merged.astype(BF16), wo_ref[...])
    ms = jnp.mean(out * out, axis=-1, keepdims=True)
    resid = x_ref[...]
    if x_unpadded and row0 == 0:
        resid = _with_front_rows(resid, front_ref, t == 0)
    res = resid + out * lax.rsqrt(ms + NORM_EPS) * np_ref[...]
    if row0 < PAD_FRONT:
        res = jnp.where((row0 + t * rows + _iota(res.shape, 0)) >= PAD_FRONT, res, 0.0)
    o_ref[...] = res


def _out(yssd, ydn, main, x, front, sc_w, w_branch, w_out, norm_post, batch, tp, rt, row0, x_unpadded):
    nt = (tp - row0) // rt
    rows = batch * nt * rt
    _, _, const, _ = _row_specs(rt, nt)
    if row0 == 0:
        tile, halo, _, _ = _row_specs(rt, nt)
    else:
        def tile(width, col):
            return pl.BlockSpec(
                (pl.Element(rt), pl.Element(width)),
                lambda b, t: ((b * (tp // HALO) + row0 // HALO + t * (rt // HALO)) * HALO, col * width))

        def halo(width, col):
            return pl.BlockSpec(
                (pl.Element(HALO), pl.Element(width)),
                lambda b, t: ((b * (tp // HALO) + row0 // HALO + t * (rt // HALO) - 1) * HALO, col * width))

    out_spec = pl.BlockSpec((rt, D_MODEL), lambda b, t: (b * nt + t, 0))
    if not x_unpadded:
        x_spec = tile(D_MODEL, 0)
    elif row0 == 0:
        x_spec = _unpadded_row_spec(rt, nt, tp - CHUNK)(lambda b, t: (b, t))
    else:
        x_spec = out_spec
    return pl.pallas_call(
        functools.partial(_out_kernel, row0=row0, x_unpadded=x_unpadded),
        grid=(batch, nt),
        in_specs=[
            tile(D_MODEL, 0), tile(D_MODEL, 0),
            tile(D_MODEL, COL_SC_B), tile(D_MODEL, COL_SC_C), halo(D_MODEL, COL_SC_C),
            tile(D_MODEL, COL_SC_H), halo(D_MODEL, COL_SC_H), tile(D_MODEL, COL_SC_G),
            tile(3 * D_MODEL, COL_GATES), x_spec, const((CHUNK, D_MODEL)),
            const((3, D_MODEL)), const((3, D_MODEL, D_MODEL)), const((D_MODEL, D_MODEL)),
            const((1, D_MODEL)),
        ],
        out_specs=out_spec,
        out_shape=jax.ShapeDtypeStruct((rows, D_MODEL), F32),
        compiler_params=pltpu.CompilerParams(
            dimension_semantics=("arbitrary", "arbitrary"), vmem_limit_bytes=VMEM_LIMIT),
        name="out_proj",
    )(yssd, ydn, main, main, main, main, main, main, main, x, front, sc_w, w_branch, w_out, norm_post)


def _lane_vec(rows):
    padded = [jnp.pad(v.astype(F32), (off, LANES - off - v.shape[0])) for off, v in rows]
    padded += [jnp.zeros((LANES,), F32)] * (8 - len(padded))
    return jnp.stack(padded)


def _expand_matrix(lane0, heads, width):
    r = jnp.arange(LANES)[:, None]
    c = jnp.arange(heads * width)[None, :]
    return (r == lane0 + c // width).astype(BF16)


def _shift_matrix(rows):
    r = jnp.arange((CONV_TAPS - 1) * rows)[:, None]
    c = jnp.arange(HALO + rows)[None, :]
    return (c == HALO - (CONV_TAPS - 1) + r % rows + r // rows).astype(BF16)


def _split_w_in(w):
    gates = w[:, IN_OFFS[11]:IN_OFFS[12]]
    qkv_dnz = w[:, IN_OFFS[7]:IN_OFFS[9]]
    sc = w[:, IN_OFFS[3]:IN_OFFS[7]]
    ssdz_xbc = w[:, IN_OFFS[0]:IN_OFFS[2]]
    return jnp.concatenate([gates, qkv_dnz, sc, ssdz_xbc], axis=1).astype(BF16)


def kernel(x, meta_tokens, norm_pre, norm_post, w_in, ssd_conv_w, ssd_conv_b, ssd_dt_bias, ssd_a_log,
           ssd_d, ssd_norm, sc_conv_w, dn_conv_w, dn_dt_bias, dn_a_log, dn_norm, w_branch, w_out):
    batch, seq, d = x.shape
    assert d == D_MODEL and seq % CHUNK == 0
    tp = CHUNK + seq
    nc = tp // CHUNK
    chunks = 3 if nc % 3 == 0 else 1
    rows = batch * tp
    tm = tp // 2 if tp % 32 == 0 else tp
    rt_out = tp // 4 if tp % (4 * HALO) == 0 else tp
    rt_last = seq // 4 if seq % (4 * CHUNK) == 0 else CHUNK
    depth = w_in.shape[0]

    front = jnp.concatenate([jnp.zeros((PAD_FRONT, d), x.dtype), meta_tokens.astype(x.dtype)], axis=0)
    h = x.reshape(batch * seq, d)

    e16 = _expand_matrix(LANE_DT, SSD_HEADS, SSD_HEAD_DIM)
    eg = _expand_matrix(LANE_A, DN_HEADS, DN_HEAD_DIM)
    eb = _expand_matrix(LANE_BETA, DN_HEADS, DN_HEAD_DIM)
    smat = _shift_matrix(chunks * CHUNK)
    pj = jnp.arange(8)[:, None]
    pc = jnp.arange(LANES)[None, :]
    psel = ((pc // 2 == pj) & (pc < SSD_HEADS)).astype(BF16)

    for i in range(depth):
        unpadded = i == 0
        dt0 = OFF_DT // LANES * LANES
        ba0 = OFF_BETA // LANES * LANES
        main, small = _in_proj(h, front, norm_pre[i][None, :], _split_w_in(w_in[i]),
                               w_in[i, :, dt0:dt0 + LANES], w_in[i, :, ba0:ba0 + LANES], tm, tp, unpadded)

        dn_vec = _lane_vec([(LANE_A, dn_dt_bias[i]), (LANE_A, dn_a_log[i])])
        qt, rhs, kd, lmat, amat, egl = _dn_prep(
            main, small, dn_conv_w[i], dn_vec, eg, eb, smat, batch, nc, chunks)
        ni = batch * nc * DN_HEADS
        lt = jnp.transpose(lmat.reshape(ni, CHUNK, CHUNK), (1, 2, 0))
        if ni < LANES:
            lt = jnp.pad(lt, ((0, 0), (0, 0), (0, LANES - ni)))
        tmat = jnp.transpose(_dn_solve(lt)[:, :, :ni], (2, 0, 1)).astype(BF16).reshape(
            batch * nc, DN_HEADS, CHUNK, CHUNK)
        ydn = _dn_scan(qt, rhs, kd, tmat, amat, egl, main, dn_norm[i][None, :], batch, nc, chunks)

        ssd_vec = _lane_vec([(LANE_DT, ssd_dt_bias[i]), (LANE_DT, ssd_a_log[i])])
        d_exp = jnp.repeat(ssd_d[i].astype(F32), SSD_HEAD_DIM)[None, :]
        yssd = _ssd(main, small, ssd_conv_w[i], ssd_conv_b[i][None, :], ssd_vec, e16, psel, d_exp,
                    ssd_norm[i][None, :], smat, batch, nc, chunks)

        last = i == depth - 1
        h = _out(yssd, ydn, main, h, front, sc_conv_w[i], w_branch[i].astype(BF16), w_out[i].astype(BF16),
                 norm_post[i][None, :], batch, tp, rt_last if last else rt_out, CHUNK if last else 0,
                 unpadded)

    return h.reshape(batch, seq, d)
```

```python
import functools

import jax
import jax.numpy as jnp
from jax import lax
from jax.experimental import pallas as pl
from jax.experimental.pallas import tpu as pltpu

F32 = jnp.float32
BF16 = jnp.bfloat16

D_MODEL = 1024
N_META = 16
CHUNK = 64
CHUNK_SHIFT = 6
PAD_FRONT = CHUNK - N_META
NORM_EPS = 1e-6
SSD_HEADS = 16
SSD_HEAD_DIM = 64
SSD_GROUPS = 2
SSD_STATE = 128
SSD_XBC = 1536
DN_HEADS = 8
DN_HEAD_DIM = 128
CONV_TAPS = 4
LANES = 128
HALO = 16
CONV_COLS = 256
PROJ_TN = 2304
VMEM_LIMIT = 52 * 1024 * 1024

MAIN_WIDTH = 13824
COL_GATES = 0
COL_QKV = 1
COL_DN_Z = 6
COL_SC_B = 7
COL_SC_C = 8
COL_SC_H = 9
COL_SC_G = 10
COL_SSD_Z = 11
COL_XBC = 8
IN_SIZES = (1024, 1536, 16, 1024, 1024, 1024, 1024, 3072, 1024, 8, 8, 3072)
IN_OFFS = tuple(sum(IN_SIZES[:i]) for i in range(len(IN_SIZES) + 1))
OFF_DT, OFF_BETA, OFF_A = IN_OFFS[2], IN_OFFS[9], IN_OFFS[10]
LANE_DT = OFF_DT % LANES
LANE_BETA = OFF_BETA % LANES
LANE_A = OFF_A % LANES
assert OFF_BETA // LANES == OFF_A // LANES and LANE_DT + SSD_HEADS <= LANE_BETA < LANE_A


def _dot(a, b):
    return jnp.dot(a, b, preferred_element_type=F32)


def _dot_nt(a, b):
    return lax.dot_general(a, b, (((1,), (1,)), ((), ())), preferred_element_type=F32)


def _dot_tn(a, b):
    return lax.dot_general(a, b, (((0,), (0,)), ((), ())), preferred_element_type=F32)


def _split3(x):
    hi = x.astype(BF16)
    r1 = x - hi.astype(F32)
    mid = r1.astype(BF16)
    lo = (r1 - mid.astype(F32)).astype(BF16)
    return hi, mid, lo


def _dot_sel(x, sel, terms=3):
    parts = _split3(x)[:terms]
    out = _dot(parts[0], sel)
    for p in parts[1:]:
        out = out + _dot(p, sel)
    return out


def _sel_dot(sel, x):
    hi, mid, lo = _split3(x)
    return _dot(sel, hi) + _dot(sel, mid) + _dot(sel, lo)


def _sel_dot_nt(sel, x):
    hi, mid, lo = _split3(x)
    return _dot_nt(sel, hi) + _dot_nt(sel, mid) + _dot_nt(sel, lo)


def _sigmoid(x):
    return 1.0 / (1.0 + jnp.exp(-x))


def _silu(x):
    return x * _sigmoid(x)


def _softplus(x):
    return jnp.maximum(x, 0.0) + jnp.log(1.0 + jnp.exp(-jnp.abs(x)))


def _iota(shape, dim):
    return lax.broadcasted_iota(jnp.int32, shape, dim)


def _chunk_tril_bf16(rows):
    r = _iota((rows, rows), 0)
    c = _iota((rows, rows), 1)
    same = lax.shift_right_logical(r, CHUNK_SHIFT) == lax.shift_right_logical(c, CHUNK_SHIFT)
    return ((r >= c) & same).astype(BF16)


def _eye_bf16(n=LANES):
    return (_iota((n, n), 0) == _iota((n, n), 1)).astype(BF16)


def _last_row_of_chunks(x, chunks):
    return jnp.concatenate(
        [jnp.broadcast_to(x[(c + 1) * CHUNK - 1:(c + 1) * CHUNK, :], (CHUNK, x.shape[1]))
         for c in range(chunks)], axis=0)


def _stage_rows(x_ref, halo_ref, xe_ref, first):
    rows = x_ref.shape[0]
    halo = halo_ref[...]
    xe_ref[0:HALO, :] = jnp.where(first, jnp.zeros_like(halo), halo)
    xe_ref[HALO:HALO + rows, :] = x_ref[...]


def _conv_blocks(xe_ref, smat_ref, w_ref, width, rows):
    slices = [slice(j * CONV_COLS, (j + 1) * CONV_COLS) for j in range(width // CONV_COLS)]
    sh_next = _dot(smat_ref[...], xe_ref[:, slices[0]])
    for j, cs in enumerate(slices):
        sh = sh_next
        if j + 1 < len(slices):
            sh_next = _dot(smat_ref[...], xe_ref[:, slices[j + 1]])
        acc = w_ref[CONV_TAPS - 1:CONV_TAPS, cs] * xe_ref[HALO:HALO + rows, cs].astype(F32)
        for k in range(CONV_TAPS - 1):
            acc = acc + w_ref[k:k + 1, cs] * sh[k * rows:(k + 1) * rows]
        yield cs, acc


def _with_front_rows(x, front_ref, is_first):
    shifted = jnp.concatenate([front_ref[...], x[:x.shape[0] - CHUNK]], axis=0)
    return jnp.where(is_first, shifted, x)


def _unpadded_row_spec(rt, tiles_per_batch, seq):
    def index(b, t):
        row = b * (seq // HALO) + jnp.maximum(t * (rt // HALO) - CHUNK // HALO, 0)
        return row * HALO, 0

    return lambda grid_to_bt: pl.BlockSpec(
        (pl.Element(rt), pl.Element(D_MODEL)), lambda *g: index(*grid_to_bt(*g)))


def _in_proj_kernel(x_ref, front_ref, gain_ref, w_ref, wdt_ref, wba_ref, main_ref, small_ref, xn_ref,
                    *, unpadded, tiles_per_batch):
    @pl.when(pl.program_id(1) == 0)
    def _():
        x = x_ref[...]
        if unpadded:
            x = _with_front_rows(x, front_ref, pl.program_id(0) % tiles_per_batch == 0)
        ms = jnp.mean(x * x, axis=-1, keepdims=True)
        xn = (x * lax.rsqrt(ms + NORM_EPS) * gain_ref[...]).astype(BF16)
        xn_ref[...] = xn
        lane = _iota(wdt_ref.shape, 1)
        ws = jnp.where(lane < LANE_DT + SSD_HEADS, wdt_ref[...],
                       jnp.where((lane >= LANE_BETA) & (lane < LANE_A + DN_HEADS), wba_ref[...], 0.0))
        small_ref[...] = _dot(xn, ws.astype(BF16))

    main_ref[...] = _dot(xn_ref[...], w_ref[...]).astype(BF16)


def _in_proj(h, front, gain, w_main, w_dt_blk, w_ba_blk, tm, tp, unpadded):
    tiles_per_batch = tp // tm
    rows = h.shape[0] // (tp - CHUNK) * tp if unpadded else h.shape[0]
    grid = (rows // tm, MAIN_WIDTH // PROJ_TN)
    if unpadded:
        x_spec = _unpadded_row_spec(tm, tiles_per_batch, tp - CHUNK)(
            lambda i, j: (i // tiles_per_batch, i % tiles_per_batch))
    else:
        x_spec = pl.BlockSpec((tm, D_MODEL), lambda i, j: (i, 0))
    return pl.pallas_call(
        functools.partial(_in_proj_kernel, unpadded=unpadded, tiles_per_batch=tiles_per_batch),
        grid=grid,
        in_specs=[
            x_spec,
            pl.BlockSpec((CHUNK, D_MODEL), lambda i, j: (0, 0)),
            pl.BlockSpec((1, D_MODEL), lambda i, j: (0, 0)),
            pl.BlockSpec((D_MODEL, PROJ_TN), lambda i, j: (0, j)),
            pl.BlockSpec((D_MODEL, LANES), lambda i, j: (0, 0)),
            pl.BlockSpec((D_MODEL, LANES), lambda i, j: (0, 0)),
        ],
        out_specs=[
            pl.BlockSpec((tm, PROJ_TN), lambda i, j: (i, j)),
            pl.BlockSpec((tm, LANES), lambda i, j: (i, 0)),
        ],
        out_shape=[
            jax.ShapeDtypeStruct((rows, MAIN_WIDTH), BF16),
            jax.ShapeDtypeStruct((rows, LANES), F32),
        ],
        scratch_shapes=[pltpu.VMEM((tm, D_MODEL), BF16)],
        compiler_params=pltpu.CompilerParams(
            dimension_semantics=("arbitrary", "arbitrary"), vmem_limit_bytes=VMEM_LIMIT),
        name="in_proj",
    )(h, front, gain, w_main, w_dt_blk, w_ba_blk)


def _row_specs(rt, nt):
    def tile(width, col):
        return pl.BlockSpec((rt, width), lambda b, t: (b * nt + t, col))

    def halo(width, col):
        return pl.BlockSpec(
            (HALO, width), lambda b, t: (jnp.maximum((b * nt + t) * (rt // HALO) - 1, 0), col))

    def const(shape):
        return pl.BlockSpec(shape, lambda b, t: (0,) * len(shape), pipeline_mode=pl.Buffered(1))

    def inst(*tail):
        shape = (rt // CHUNK, DN_HEADS) + tail
        return pl.BlockSpec(shape, lambda b, t: (b * nt + t,) + (0,) * (len(shape) - 1))

    return tile, halo, const, inst


def _dn_prep_kernel(qkv_ref, halo_ref, small_ref, cw_ref, vec_ref, eg_ref, eb_ref, smat_ref,
                    qt_ref, rhs_ref, kd_ref, l_ref, a_ref, egl_ref, xe_ref, *, chunks):
    t = pl.program_id(1)
    rows = chunks * CHUNK
    _stage_rows(qkv_ref, halo_ref, xe_ref, t == 0)

    sm = small_ref[...]
    lane = _iota(sm.shape, 1)
    real = (t * rows + _iota(sm.shape, 0)) >= PAD_FRONT
    is_beta = (lane >= LANE_BETA) & (lane < LANE_BETA + DN_HEADS)
    is_a = (lane >= LANE_A) & (lane < LANE_A + DN_HEADS)
    g = -jnp.exp(vec_ref[1:2, :]) * _softplus(sm + vec_ref[0:1, :])
    g = jnp.where(real & is_a, g, 0.0)
    beta = jnp.where(real & is_beta, _sigmoid(sm), 0.0)
    gcum = _sel_dot(_chunk_tril_bf16(rows), g)
    ge = _dot_sel(gcum, eg_ref[...])
    be = _dot_sel(beta, eb_ref[...], terms=2)
    gl = _last_row_of_chunks(ge, chunks)
    eg = jnp.exp(ge)
    ekd = jnp.exp(gl - ge)

    li = _iota((CHUNK, CHUNK), 0)
    si = _iota((CHUNK, CHUNK), 1)
    causal = li >= si
    strict = li > si
    eye = _eye_bf16()
    cr = [slice(c * CHUNK, (c + 1) * CHUNK) for c in range(chunks)]
    egl = [jnp.exp(ge[cr[c].stop - 1:cr[c].stop, :]) for c in range(chunks)]
    gct = [_sel_dot_nt(eye, gcum[cr[c]]) for c in range(chunks)]

    def l2n(vh, scale):
        ss = jnp.sum(vh * vh, axis=-1, keepdims=True)
        return vh * (lax.rsqrt(ss + NORM_EPS) * scale)

    heads_per_block = CONV_COLS // DN_HEAD_DIM
    qkv = []
    for j, (cs, acc) in enumerate(_conv_blocks(xe_ref, smat_ref, cw_ref, 3 * D_MODEL, rows)):
        val = _silu(acc)
        for i in range(heads_per_block):
            vh = val[:, i * DN_HEAD_DIM:(i + 1) * DN_HEAD_DIM]
            if j < D_MODEL // CONV_COLS:
                vh = l2n(vh, DN_HEAD_DIM ** -0.5)
            elif j < 2 * D_MODEL // CONV_COLS:
                vh = l2n(vh, 1.0)
            qkv.append(vh)
    qs, ks, vs = qkv[:DN_HEADS], qkv[DN_HEADS:2 * DN_HEADS], qkv[2 * DN_HEADS:]

    for h in range(DN_HEADS):
        hs = slice(h * DN_HEAD_DIM, (h + 1) * DN_HEAD_DIM)
        bt = be[:, hs]
        qt_ref[:, hs] = (qs[h] * eg[:, hs]).astype(BF16)
        rhs_ref[:, 2 * h * DN_HEAD_DIM:(2 * h + 1) * DN_HEAD_DIM] = (vs[h] * bt).astype(BF16)
        rhs_ref[:, (2 * h + 1) * DN_HEAD_DIM:(2 * h + 2) * DN_HEAD_DIM] = (ks[h] * (bt * eg[:, hs])).astype(BF16)
        kd_ref[:, hs] = (ks[h] * ekd[:, hs]).astype(BF16)

    for c in range(chunks):
        for h in range(DN_HEADS):
            c0 = h * DN_HEAD_DIM
            kb = ks[h][cr[c]].astype(BF16)
            kq = _dot_nt(jnp.concatenate([kb, qs[h][cr[c]].astype(BF16)], axis=0), kb)
            gs = gct[c][LANE_A + h:LANE_A + h + 1, :]
            decay = jnp.where(causal, jnp.exp(jnp.minimum(ge[cr[c], c0:c0 + CHUNK] - gs, 0.0)), 0.0)
            l_ref[c, h] = jnp.where(strict, be[cr[c], c0:c0 + CHUNK] * kq[:CHUNK] * decay, 0.0)
            a_ref[c, h] = (kq[CHUNK:] * decay).astype(BF16)
            egl_ref[c, h:h + 1, :] = egl[c][:, c0:c0 + LANES]


def _dn_prep(main, small, conv_w, vec, eg, eb, smat, batch, nc, chunks):
    rows = main.shape[0]
    nt = nc // chunks
    rt = chunks * CHUNK
    tile, halo, const, inst = _row_specs(rt, nt)
    return pl.pallas_call(
        functools.partial(_dn_prep_kernel, chunks=chunks),
        grid=(batch, nt),
        in_specs=[
            tile(3 * D_MODEL, COL_QKV), halo(3 * D_MODEL, COL_QKV), tile(LANES, 0),
            const((CONV_TAPS, 3 * D_MODEL)), const((8, LANES)),
            const((LANES, D_MODEL)), const((LANES, D_MODEL)), const(smat.shape),
        ],
        out_specs=[
            tile(D_MODEL, 0), tile(2 * D_MODEL, 0), tile(D_MODEL, 0),
            inst(CHUNK, CHUNK), inst(CHUNK, CHUNK), inst(LANES),
        ],
        out_shape=[
            jax.ShapeDtypeStruct((rows, D_MODEL), BF16),
            jax.ShapeDtypeStruct((rows, 2 * D_MODEL), BF16),
            jax.ShapeDtypeStruct((rows, D_MODEL), BF16),
            jax.ShapeDtypeStruct((batch * nc, DN_HEADS, CHUNK, CHUNK), F32),
            jax.ShapeDtypeStruct((batch * nc, DN_HEADS, CHUNK, CHUNK), BF16),
            jax.ShapeDtypeStruct((batch * nc, DN_HEADS, LANES), F32),
        ],
        scratch_shapes=[pltpu.VMEM((HALO + rt, 3 * D_MODEL), BF16)],
        compiler_params=pltpu.CompilerParams(
            dimension_semantics=("arbitrary", "arbitrary"), vmem_limit_bytes=VMEM_LIMIT),
        name="dn_prep",
    )(main, main, small, conv_w, vec, eg, eb, smat)


def _dn_solve_kernel(lt_ref, tt_ref):
    sub = 8
    nblk = CHUNK // sub
    tt_ref[...] = jnp.zeros_like(tt_ref)
    col = _iota((sub, LANES), 0)

    for ib in range(nblk):
        def row_body(i, carry, ib=ib):
            acc = [(col + k * sub == i).astype(F32) for k in range(ib + 1)]
            for jb in range(ib + 1):
                blk = lt_ref[i, jb * sub:(jb + 1) * sub, :]
                for jj in range(sub):
                    j = jb * sub + jj
                    lij = blk[jj:jj + 1, :]
                    for k in range(jb + 1):
                        acc[k] = acc[k] - lij * tt_ref[j, k * sub:(k + 1) * sub, :]
            tt_ref[i, 0:(ib + 1) * sub, :] = jnp.concatenate(acc, axis=0)
            return carry

        lax.fori_loop(ib * sub, (ib + 1) * sub, row_body, 0)


def _dn_solve(lt):
    ni = lt.shape[-1]
    spec = pl.BlockSpec((CHUNK, CHUNK, LANES), lambda i: (0, 0, i))
    return pl.pallas_call(
        _dn_solve_kernel,
        grid=(pl.cdiv(ni, LANES),),
        in_specs=[spec],
        out_specs=spec,
        out_shape=jax.ShapeDtypeStruct(lt.shape, F32),
        compiler_params=pltpu.CompilerParams(dimension_semantics=("arbitrary",)),
        name="dn_solve",
    )(lt)


def _dn_scan_kernel(qt_ref, rhs_ref, kd_ref, t_ref, a_ref, egl_ref, z_ref, nw_ref, y_ref, s_ref,
                    *, chunks, nb):
    @pl.when(pl.program_id(1) == 0)
    def _():
        s_ref[...] = jnp.zeros_like(s_ref)

    nw = nw_ref[...]
    units = [(b, h) for b in range(nb) for h in range(DN_HEADS)]
    uw = [[_dot(t_ref[b, c, h],
                rhs_ref[b, c * CHUNK:(c + 1) * CHUNK, 2 * h * DN_HEAD_DIM:(2 * h + 2) * DN_HEAD_DIM])
           for (b, h) in units] for c in range(chunks)]
    s = [s_ref[b * DN_HEADS + h] for (b, h) in units]
    for c in range(chunks):
        rc = slice(c * CHUNK, (c + 1) * CHUNK)
        ws = []
        for i, (b, h) in enumerate(units):
            hs = slice(h * DN_HEAD_DIM, (h + 1) * DN_HEAD_DIM)
            wq = jnp.concatenate([uw[c][i][:, DN_HEAD_DIM:].astype(BF16), qt_ref[b, rc, hs]], axis=0)
            ws.append(_dot(wq, s[i].astype(BF16)))
        for i, (b, h) in enumerate(units):
            hs = slice(h * DN_HEAD_DIM, (h + 1) * DN_HEAD_DIM)
            vn = (uw[c][i][:, :DN_HEAD_DIM] - ws[i][:CHUNK]).astype(BF16)
            o = ws[i][CHUNK:] + _dot(a_ref[b, c, h], vn)
            s[i] = s[i] * egl_ref[b, c, h:h + 1, :] + _dot_tn(kd_ref[b, rc, hs], vn)
            ms = jnp.mean(o * o, axis=-1, keepdims=True)
            on = o * lax.rsqrt(ms + NORM_EPS) * nw
            y_ref[b, rc, hs] = (on * _silu(z_ref[b, rc, hs].astype(F32))).astype(BF16)
    for i, (b, h) in enumerate(units):
        s_ref[b * DN_HEADS + h] = s[i]


def _dn_scan(qt, rhs, kd, tmat, amat, egl, main, norm_w, batch, nc, chunks):
    tp = nc * CHUNK
    nt = nc // chunks
    rt = chunks * CHUNK
    nb = 2 if batch % 2 == 0 else 1

    def seq_tile(width, col):
        return pl.BlockSpec((nb, rt, width), lambda p, t: (p, t, col))

    def inst(*tail):
        shape = (nb, chunks, DN_HEADS) + tail
        return pl.BlockSpec(shape, lambda p, t: (p, t) + (0,) * (len(shape) - 2))

    def by_seq(a):
        return a.reshape((batch, a.shape[0] // batch) + a.shape[1:])

    y = pl.pallas_call(
        functools.partial(_dn_scan_kernel, chunks=chunks, nb=nb),
        grid=(batch // nb, nt),
        in_specs=[
            seq_tile(D_MODEL, 0), seq_tile(2 * D_MODEL, 0), seq_tile(D_MODEL, 0), inst(CHUNK, CHUNK),
            inst(CHUNK, CHUNK), inst(LANES), seq_tile(D_MODEL, COL_DN_Z),
            pl.BlockSpec((1, DN_HEAD_DIM), lambda p, t: (0, 0), pipeline_mode=pl.Buffered(1)),
        ],
        out_specs=seq_tile(D_MODEL, 0),
        out_shape=jax.ShapeDtypeStruct((batch, tp, D_MODEL), BF16),
        scratch_shapes=[pltpu.VMEM((nb * DN_HEADS, DN_HEAD_DIM, DN_HEAD_DIM), F32)],
        compiler_params=pltpu.CompilerParams(
            dimension_semantics=("arbitrary", "arbitrary"), vmem_limit_bytes=VMEM_LIMIT),
        name="dn_scan",
    )(by_seq(qt), by_seq(rhs), by_seq(kd), by_seq(tmat), by_seq(amat), by_seq(egl), by_seq(main), norm_w)
    return y.reshape(batch * tp, D_MODEL)


def _ssd_kernel(xbc_ref, halo_ref, z_ref, small_ref, cw_ref, cb_ref, vec_ref, e16_ref, psel_ref,
                de_ref, nw_ref, smat_ref, y_ref, xe_ref, h_ref, *, chunks, nb):
    t = pl.program_id(1)
    rows = chunks * CHUNK
    nch = nb * chunks

    @pl.when(t == 0)
    def _():
        h_ref[...] = jnp.zeros_like(h_ref)

    for b in range(nb):
        _stage_rows(xbc_ref.at[b], halo_ref.at[b], xe_ref.at[b], t == 0)

    sm = jnp.concatenate([small_ref[b] for b in range(nb)], axis=0)
    lane = _iota((rows, LANES), 1)
    real = ((t * rows + _iota((rows, LANES), 0)) >= PAD_FRONT) & (lane < SSD_HEADS)
    dt = jnp.where(jnp.concatenate([real] * nb, axis=0), _softplus(sm + vec_ref[0:1, :]), 0.0)
    da = dt * (-jnp.exp(vec_ref[1:2, :]))
    tril = _chunk_tril_bf16(rows)
    acum = jnp.concatenate([_sel_dot(tril, da[b * rows:(b + 1) * rows]) for b in range(nb)], axis=0)
    ae = _dot_sel(acum, e16_ref[...])
    dte = _dot_sel(dt, e16_ref[...], terms=2)
    alast = _last_row_of_chunks(ae, nch)

    conv = [[_silu(acc + cb_ref[:, cs])
             for cs, acc in _conv_blocks(xe_ref.at[b], smat_ref, cw_ref, SSD_XBC, rows)] for b in range(nb)]
    nx = D_MODEL // CONV_COLS
    xs = jnp.concatenate([jnp.concatenate(cb[:nx], axis=1) for cb in conv], axis=0)
    bc = jnp.concatenate([jnp.concatenate(cb[nx:], axis=1) for cb in conv], axis=0).astype(BF16)
    xc = xs * dte
    xcb = xc.astype(BF16)
    xdec = (xc * jnp.exp(alast - ae)).astype(BF16)
    eae = jnp.exp(ae)
    ealast = [jnp.exp(ae[(c + 1) * CHUNK - 1:(c + 1) * CHUNK, :]) for c in range(nch)]

    half = SSD_HEADS // SSD_GROUPS * SSD_HEAD_DIM
    pairs = SSD_HEADS // SSD_GROUPS // 2
    cr = [slice(c * CHUNK, (c + 1) * CHUNK) for c in range(nch)]
    gsl = [slice(g * half, (g + 1) * half) for g in range(SSD_GROUPS)]
    bgs = [[bc[cr[c], g * SSD_STATE:(g + 1) * SSD_STATE] for g in range(SSD_GROUPS)] for c in range(nch)]
    cgs = [[bc[cr[c], (SSD_GROUPS + g) * SSD_STATE:(SSD_GROUPS + g + 1) * SSD_STATE]
            for g in range(SSD_GROUPS)] for c in range(nch)]

    states = [[_dot_tn(bgs[c][g], xdec[cr[c], gsl[g]]) for g in range(SSD_GROUPS)] for c in range(nch)]
    yoff = [[None] * SSD_GROUPS for _ in range(nch)]
    for b in range(nb):
        for g in range(SSD_GROUPS):
            hg = h_ref[b * SSD_GROUPS + g]
            for c in range(b * chunks, (b + 1) * chunks):
                yoff[c][g] = _dot(cgs[c][g], hg.astype(BF16)) * eae[cr[c], gsl[g]]
                hg = hg * ealast[c][:, gsl[g]] + states[c][g]
            h_ref[b * SSD_GROUPS + g] = hg

    psel = psel_ref[...]
    prow = _iota((2 * CHUNK, LANES), 0)
    plane = _iota((2 * CHUNK, LANES), 1)
    keep = (prow < CHUNK) == ((plane & 1) == 0)
    bd_mask = (prow < CHUNK) == (plane < CHUNK)
    causal2 = _iota((CHUNK, LANES), 0) >= (_iota((CHUNK, LANES), 1) & (CHUNK - 1))
    cbcb = [[_dot_nt(cgs[c][g], jnp.concatenate([bgs[c][g], bgs[c][g]], axis=0))
             for g in range(SSD_GROUPS)] for c in range(nch)]
    rowpair = []
    for c in range(nch):
        ac = acum[cr[c]]
        a2 = jnp.where(keep, jnp.concatenate([ac, ac], axis=0), 0.0)
        rowpair.append(_sel_dot_nt(psel, a2))
    nw = nw_ref[...]
    de = de_ref[...]
    for c in range(nch):
        b, lc = c // chunks, slice((c % chunks) * CHUNK, (c % chunks + 1) * CHUNK)
        ys = []
        for g in range(SSD_GROUPS):
            for p in range(pairs):
                j = g * pairs + p
                js = slice(j * LANES, (j + 1) * LANES)
                diff = ae[cr[c], js] - rowpair[c][j:j + 1, :]
                m = jnp.where(causal2, cbcb[c][g] * jnp.exp(jnp.minimum(diff, 0.0)), 0.0).astype(BF16)
                xp = xcb[cr[c], js]
                bd = jnp.where(bd_mask, jnp.concatenate([xp, xp], axis=0), jnp.zeros((), BF16))
                ys.append(_dot(m, bd) + yoff[c][g][:, p * LANES:(p + 1) * LANES])
        y = jnp.concatenate(ys, axis=1) + de * xs[cr[c]]
        y = y * _silu(z_ref[b, lc, :].astype(F32))
        for g in range(SSD_GROUPS):
            yg = y[:, gsl[g]]
            ms = jnp.mean(yg * yg, axis=-1, keepdims=True)
            y_ref[b, lc, gsl[g]] = (yg * lax.rsqrt(ms + NORM_EPS) * nw[:, gsl[g]]).astype(BF16)


def _ssd(main, small, conv_w, conv_b, vec, e16, psel, d_exp, norm_w, smat, batch, nc, chunks):
    tp = nc * CHUNK
    nt = nc // chunks
    rt = chunks * CHUNK
    nb = 2 if batch % 2 == 0 else 1
    _, _, const, _ = _row_specs(rt, nt)

    def seq_tile(width, col):
        return pl.BlockSpec((nb, rt, width), lambda p, t: (p, t, col))

    def seq_halo(width, col):
        return pl.BlockSpec((nb, HALO, width), lambda p, t: (p, jnp.maximum(t * (rt // HALO) - 1, 0), col))

    main3 = main.reshape(batch, tp, main.shape[1])
    y = pl.pallas_call(
        functools.partial(_ssd_kernel, chunks=chunks, nb=nb),
        grid=(batch // nb, nt),
        in_specs=[
            seq_tile(SSD_XBC, COL_XBC), seq_halo(SSD_XBC, COL_XBC), seq_tile(D_MODEL, COL_SSD_Z),
            seq_tile(LANES, 0),
            const((CONV_TAPS, SSD_XBC)), const((1, SSD_XBC)), const((8, LANES)),
            const((LANES, D_MODEL)), const((8, LANES)), const((1, D_MODEL)), const((1, D_MODEL)),
            const(smat.shape),
        ],
        out_specs=seq_tile(D_MODEL, 0),
        out_shape=jax.ShapeDtypeStruct((batch, tp, D_MODEL), BF16),
        scratch_shapes=[
            pltpu.VMEM((nb, HALO + rt, SSD_XBC), BF16),
            pltpu.VMEM((nb * SSD_GROUPS, SSD_STATE, D_MODEL // SSD_GROUPS), F32),
        ],
        compiler_params=pltpu.CompilerParams(
            dimension_semantics=("arbitrary", "arbitrary"), vmem_limit_bytes=VMEM_LIMIT),
        name="ssd",
    )(main3, main3, main3, small.reshape(batch, tp, LANES), conv_w, conv_b, vec, e16, psel, d_exp, norm_w, smat)
    return y.reshape(batch * tp, D_MODEL)


def _out_kernel(yssd_ref, ydn_ref, b_ref, c_ref, chalo_ref, h_ref, hhalo_ref, g_ref, gates_ref,
                x_ref, front_ref, scw_ref, wb_ref, wo_ref, np_ref, o_ref, *, row0, x_unpadded):
    t = pl.program_id(1)
    rows = x_ref.shape[0]
    sub = 8
    u = c_ref[...].astype(F32) * h_ref[...].astype(F32)
    uh = chalo_ref[...].astype(F32) * hhalo_ref[...].astype(F32)
    if row0 == 0:
        uh = jnp.where(t == 0, 0.0, uh)
    w0, w1, w2 = scw_ref[0:1, :], scw_ref[1:2, :], scw_ref[2:3, :]
    conv = w2 * u + w1 * pltpu.roll(u, 1, 0) + w0 * pltpu.roll(u, 2, 0)
    head = jnp.concatenate([uh, u[0:sub]], axis=0)
    fix = (w2 * head[HALO:HALO + sub] + w1 * head[HALO - 1:HALO - 1 + sub]
           + w0 * head[HALO - 2:HALO - 2 + sub])
    conv = jnp.concatenate([fix, conv[sub:]], axis=0)
    ysc = (b_ref[...].astype(F32) * conv * _silu(g_ref[...].astype(F32))).astype(BF16)

    merged = None
    for n, y in enumerate((yssd_ref[...], ysc, ydn_ref[...])):
        gate = _sigmoid(gates_ref[:, n * D_MODEL:(n + 1) * D_MODEL].astype(F32))
        term = gate * _dot(y, wb_ref[n])
        merged = term if merged is None else merged + term
    out = _dot(merged.astype(BF16), wo_ref[...])
    ms = jnp.mean(out * out, axis=-1, keepdims=True)
    resid = x_ref[...]
    if x_unpadded and row0 == 0:
        resid = _with_front_rows(resid, front_ref, t == 0)
    res = resid + out * lax.rsqrt(ms + NORM_EPS) * np_ref[...]
    if row0 < PAD_FRONT:
        res = jnp.where((row0 + t * rows + _iota(res.shape, 0)) >= PAD_FRONT, res, 0.0)
    o_ref[...] = res


def _out(yssd, ydn, main, x, front, sc_w, w_branch, w_out, norm_post, batch, tp, rt, row0, x_unpadded):
    nt = (tp - row0) // rt
    rows = batch * nt * rt
    _, _, const, _ = _row_specs(rt, nt)
    if row0 == 0:
        tile, halo, _, _ = _row_specs(rt, nt)
    else:
        def tile(width, col):
            return pl.BlockSpec(
                (pl.Element(rt), pl.Element(width)),
                lambda b, t: ((b * (tp // HALO) + row0 // HALO + t * (rt // HALO)) * HALO, col * width))

        def halo(width, col):
            return pl.BlockSpec(
                (pl.Element(HALO), pl.Element(width)),
                lambda b, t: ((b * (tp // HALO) + row0 // HALO + t * (rt // HALO) - 1) * HALO, col * width))

    out_spec = pl.BlockSpec((rt, D_MODEL), lambda b, t: (b * nt + t, 0))
    if not x_unpadded:
        x_spec = tile(D_MODEL, 0)
    elif row0 == 0:
        x_spec = _unpadded_row_spec(rt, nt, tp - CHUNK)(lambda b, t: (b, t))
    else:
        x_spec = out_spec
    return pl.pallas_call(
        functools.partial(_out_kernel, row0=row0, x_unpadded=x_unpadded),
        grid=(batch, nt),
        in_specs=[
            tile(D_MODEL, 0), tile(D_MODEL, 0),
            tile(D_MODEL, COL_SC_B), tile(D_MODEL, COL_SC_C), halo(D_MODEL, COL_SC_C),
            tile(D_MODEL, COL_SC_H), halo(D_MODEL, COL_SC_H), tile(D_MODEL, COL_SC_G),
            tile(3 * D_MODEL, COL_GATES), x_spec, const((CHUNK, D_MODEL)),
            const((3, D_MODEL)), const((3, D_MODEL, D_MODEL)), const((D_MODEL, D_MODEL)),
            const((1, D_MODEL)),
        ],
        out_specs=out_spec,
        out_shape=jax.ShapeDtypeStruct((rows, D_MODEL), F32),
        compiler_params=pltpu.CompilerParams(
            dimension_semantics=("arbitrary", "arbitrary"), vmem_limit_bytes=VMEM_LIMIT),
        name="out_proj",
    )(yssd, ydn, main, main, main, main, main, main, main, x, front, sc_w, w_branch, w_out, norm_post)


def _lane_vec(rows):
    padded = [jnp.pad(v.astype(F32), (off, LANES - off - v.shape[0])) for off, v in rows]
    padded += [jnp.zeros((LANES,), F32)] * (8 - len(padded))
    return jnp.stack(padded)


def _expand_matrix(lane0, heads, width):
    r = jnp.arange(LANES)[:, None]
    c = jnp.arange(heads * width)[None, :]
    return (r == lane0 + c // width).astype(BF16)


def _shift_matrix(rows):
    r = jnp.arange((CONV_TAPS - 1) * rows)[:, None]
    c = jnp.arange(HALO + rows)[None, :]
    return (c == HALO - (CONV_TAPS - 1) + r % rows + r // rows).astype(BF16)


def _split_w_in(w):
    gates = w[:, IN_OFFS[11]:IN_OFFS[12]]
    qkv_dnz = w[:, IN_OFFS[7]:IN_OFFS[9]]
    sc = w[:, IN_OFFS[3]:IN_OFFS[7]]
    ssdz_xbc = w[:, IN_OFFS[0]:IN_OFFS[2]]
    return jnp.concatenate([gates, qkv_dnz, sc, ssdz_xbc], axis=1).astype(BF16)


def kernel(x, meta_tokens, norm_pre, norm_post, w_in, ssd_conv_w, ssd_conv_b, ssd_dt_bias, ssd_a_log,
           ssd_d, ssd_norm, sc_conv_w, dn_conv_w, dn_dt_bias, dn_a_log, dn_norm, w_branch, w_out):
    batch, seq, d = x.shape
    assert d == D_MODEL and seq % CHUNK == 0
    tp = CHUNK + seq
    nc = tp // CHUNK
    chunks = 3 if nc % 3 == 0 else 1
    rows = batch * tp
    tm = tp // 2 if tp % 32 == 0 else tp
    rt_out = tp // 4 if tp % (4 * HALO) == 0 else tp
    rt_last = seq // 4 if seq % (4 * CHUNK) == 0 else CHUNK
    depth = w_in.shape[0]

    front = jnp.concatenate([jnp.zeros((PAD_FRONT, d), x.dtype), meta_tokens.astype(x.dtype)], axis=0)
    h = x.reshape(batch * seq, d)

    e16 = _expand_matrix(LANE_DT, SSD_HEADS, SSD_HEAD_DIM)
    eg = _expand_matrix(LANE_A, DN_HEADS, DN_HEAD_DIM)
    eb = _expand_matrix(LANE_BETA, DN_HEADS, DN_HEAD_DIM)
    smat = _shift_matrix(chunks * CHUNK)
    pj = jnp.arange(8)[:, None]
    pc = jnp.arange(LANES)[None, :]
    psel = ((pc // 2 == pj) & (pc < SSD_HEADS)).astype(BF16)

    for i in range(depth):
        unpadded = i == 0
        dt0 = OFF_DT // LANES * LANES
        ba0 = OFF_BETA // LANES * LANES
        main, small = _in_proj(h, front, norm_pre[i][None, :], _split_w_in(w_in[i]),
                               w_in[i, :, dt0:dt0 + LANES], w_in[i, :, ba0:ba0 + LANES], tm, tp, unpadded)

        dn_vec = _lane_vec([(LANE_A, dn_dt_bias[i]), (LANE_A, dn_a_log[i])])
        qt, rhs, kd, lmat, amat, egl = _dn_prep(
            main, small, dn_conv_w[i], dn_vec, eg, eb, smat, batch, nc, chunks)
        ni = batch * nc * DN_HEADS
        lt = jnp.transpose(lmat.reshape(ni, CHUNK, CHUNK), (1, 2, 0))
        if ni < LANES:
            lt = jnp.pad(lt, ((0, 0), (0, 0), (0, LANES - ni)))

        ssd_vec = _lane_vec([(LANE_DT, ssd_dt_bias[i]), (LANE_DT, ssd_a_log[i])])
        d_exp = jnp.repeat(ssd_d[i].astype(F32), SSD_HEAD_DIM)[None, :]
        yssd = _ssd(main, small, ssd_conv_w[i], ssd_conv_b[i][None, :], ssd_vec, e16, psel, d_exp,
                    ssd_norm[i][None, :], smat, batch, nc, chunks)

        tmat = jnp.transpose(_dn_solve(lt)[:, :, :ni], (2, 0, 1)).astype(BF16).reshape(
            batch * nc, DN_HEADS, CHUNK, CHUNK)
        ydn = _dn_scan(qt, rhs, kd, tmat, amat, egl, main, dn_norm[i][None, :], batch, nc, chunks)

        last = i == depth - 1
        h = _out(yssd, ydn, main, h, front, sc_conv_w[i], w_branch[i].astype(BF16), w_out[i].astype(BF16),
                 norm_post[i][None, :], batch, tp, rt_last if last else rt_out, CHUNK if last else 0,
                 unpadded)

    return h.reshape(batch, seq, d)
```

```python
import functools

import jax
import jax.numpy as jnp
from jax import lax
from jax.experimental import pallas as pl
from jax.experimental.pallas import tpu as pltpu

F32 = jnp.float32
BF16 = jnp.bfloat16

D_MODEL = 1024
N_META = 16
CHUNK = 64
CHUNK_SHIFT = 6
PAD_FRONT = CHUNK - N_META
NORM_EPS = 1e-6
SSD_HEADS = 16
SSD_HEAD_DIM = 64
SSD_GROUPS = 2
SSD_STATE = 128
SSD_XBC = 1536
DN_HEADS = 8
DN_HEAD_DIM = 128
CONV_TAPS = 4
LANES = 128
HALO = 16
CONV_COLS = 256
PROJ_TN = 2304
VMEM_LIMIT = 52 * 1024 * 1024

MAIN_WIDTH = 13824
COL_GATES = 0
COL_QKV = 1
COL_DN_Z = 6
COL_SC_B = 7
COL_SC_C = 8
COL_SC_H = 9
COL_SC_G = 10
COL_SSD_Z = 11
COL_XBC = 8
IN_SIZES = (1024, 1536, 16, 1024, 1024, 1024, 1024, 3072, 1024, 8, 8, 3072)
IN_OFFS = tuple(sum(IN_SIZES[:i]) for i in range(len(IN_SIZES) + 1))
OFF_DT, OFF_BETA, OFF_A = IN_OFFS[2], IN_OFFS[9], IN_OFFS[10]
LANE_DT = OFF_DT % LANES
LANE_BETA = OFF_BETA % LANES
LANE_A = OFF_A % LANES
assert OFF_BETA // LANES == OFF_A // LANES and LANE_DT + SSD_HEADS <= LANE_BETA < LANE_A


def _dot(a, b):
    return jnp.dot(a, b, preferred_element_type=F32)


def _dot_nt(a, b):
    return lax.dot_general(a, b, (((1,), (1,)), ((), ())), preferred_element_type=F32)


def _dot_tn(a, b):
    return lax.dot_general(a, b, (((0,), (0,)), ((), ())), preferred_element_type=F32)


def _split3(x):
    hi = x.astype(BF16)
    r1 = x - hi.astype(F32)
    mid = r1.astype(BF16)
    lo = (r1 - mid.astype(F32)).astype(BF16)
    return hi, mid, lo


def _dot_sel(x, sel, terms=3):
    parts = _split3(x)[:terms]
    out = _dot(parts[0], sel)
    for p in parts[1:]:
        out = out + _dot(p, sel)
    return out


def _sel_dot(sel, x):
    hi, mid, lo = _split3(x)
    return _dot(sel, hi) + _dot(sel, mid) + _dot(sel, lo)


def _sel_dot_nt(sel, x):
    hi, mid, lo = _split3(x)
    return _dot_nt(sel, hi) + _dot_nt(sel, mid) + _dot_nt(sel, lo)


def _sigmoid(x):
    return 1.0 / (1.0 + jnp.exp(-x))


def _silu(x):
    return x * _sigmoid(x)


def _softplus(x):
    return jnp.maximum(x, 0.0) + jnp.log(1.0 + jnp.exp(-jnp.abs(x)))


def _iota(shape, dim):
    return lax.broadcasted_iota(jnp.int32, shape, dim)


def _chunk_tril_bf16(rows):
    r = _iota((rows, rows), 0)
    c = _iota((rows, rows), 1)
    same = lax.shift_right_logical(r, CHUNK_SHIFT) == lax.shift_right_logical(c, CHUNK_SHIFT)
    return ((r >= c) & same).astype(BF16)


def _eye_bf16(n=LANES):
    return (_iota((n, n), 0) == _iota((n, n), 1)).astype(BF16)


def _last_row_of_chunks(x, chunks):
    return jnp.concatenate(
        [jnp.broadcast_to(x[(c + 1) * CHUNK - 1:(c + 1) * CHUNK, :], (CHUNK, x.shape[1]))
         for c in range(chunks)], axis=0)


def _stage_rows(x_ref, halo_ref, xe_ref, first):
    rows = x_ref.shape[0]
    halo = halo_ref[...]
    xe_ref[0:HALO, :] = jnp.where(first, jnp.zeros_like(halo), halo)
    xe_ref[HALO:HALO + rows, :] = x_ref[...]


def _conv_blocks(xe_ref, smat_ref, w_ref, width, rows):
    slices = [slice(j * CONV_COLS, (j + 1) * CONV_COLS) for j in range(width // CONV_COLS)]
    sh_next = _dot(smat_ref[...], xe_ref[:, slices[0]])
    for j, cs in enumerate(slices):
        sh = sh_next
        if j + 1 < len(slices):
            sh_next = _dot(smat_ref[...], xe_ref[:, slices[j + 1]])
        acc = w_ref[CONV_TAPS - 1:CONV_TAPS, cs] * xe_ref[HALO:HALO + rows, cs].astype(F32)
        for k in range(CONV_TAPS - 1):
            acc = acc + w_ref[k:k + 1, cs] * sh[k * rows:(k + 1) * rows]
        yield cs, acc


def _with_front_rows(x, front_ref, is_first):
    shifted = jnp.concatenate([front_ref[...], x[:x.shape[0] - CHUNK]], axis=0)
    return jnp.where(is_first, shifted, x)


def _unpadded_row_spec(rt, tiles_per_batch, seq):
    def index(b, t):
        row = b * (seq // HALO) + jnp.maximum(t * (rt // HALO) - CHUNK // HALO, 0)
        return row * HALO, 0

    return lambda grid_to_bt: pl.BlockSpec(
        (pl.Element(rt), pl.Element(D_MODEL)), lambda *g: index(*grid_to_bt(*g)))


def _in_proj_kernel(x_ref, front_ref, gain_ref, w_ref, wdt_ref, wba_ref, main_ref, small_ref, xn_ref,
                    *, unpadded, tiles_per_batch):
    @pl.when(pl.program_id(1) == 0)
    def _():
        x = x_ref[...]
        if unpadded:
            x = _with_front_rows(x, front_ref, pl.program_id(0) % tiles_per_batch == 0)
        ms = jnp.mean(x * x, axis=-1, keepdims=True)
        xn = (x * lax.rsqrt(ms + NORM_EPS) * gain_ref[...]).astype(BF16)
        xn_ref[...] = xn
        lane = _iota(wdt_ref.shape, 1)
        ws = jnp.where(lane < LANE_DT + SSD_HEADS, wdt_ref[...],
                       jnp.where((lane >= LANE_BETA) & (lane < LANE_A + DN_HEADS), wba_ref[...], 0.0))
        small_ref[...] = _dot(xn, ws.astype(BF16))

    main_ref[...] = _dot(xn_ref[...], w_ref[...]).astype(BF16)


def _in_proj(h, front, gain, w_main, w_dt_blk, w_ba_blk, tm, tp, unpadded):
    tiles_per_batch = tp // tm
    rows = h.shape[0] // (tp - CHUNK) * tp if unpadded else h.shape[0]
    grid = (rows // tm, MAIN_WIDTH // PROJ_TN)
    if unpadded:
        x_spec = _unpadded_row_spec(tm, tiles_per_batch, tp - CHUNK)(
            lambda i, j: (i // tiles_per_batch, i % tiles_per_batch))
    else:
        x_spec = pl.BlockSpec((tm, D_MODEL), lambda i, j: (i, 0))
    return pl.pallas_call(
        functools.partial(_in_proj_kernel, unpadded=unpadded, tiles_per_batch=tiles_per_batch),
        grid=grid,
        in_specs=[
            x_spec,
            pl.BlockSpec((CHUNK, D_MODEL), lambda i, j: (0, 0)),
            pl.BlockSpec((1, D_MODEL), lambda i, j: (0, 0)),
            pl.BlockSpec((D_MODEL, PROJ_TN), lambda i, j: (0, j)),
            pl.BlockSpec((D_MODEL, LANES), lambda i, j: (0, 0)),
            pl.BlockSpec((D_MODEL, LANES), lambda i, j: (0, 0)),
        ],
        out_specs=[
            pl.BlockSpec((tm, PROJ_TN), lambda i, j: (i, j)),
            pl.BlockSpec((tm, LANES), lambda i, j: (i, 0)),
        ],
        out_shape=[
            jax.ShapeDtypeStruct((rows, MAIN_WIDTH), BF16),
            jax.ShapeDtypeStruct((rows, LANES), F32),
        ],
        scratch_shapes=[pltpu.VMEM((tm, D_MODEL), BF16)],
        compiler_params=pltpu.CompilerParams(
            dimension_semantics=("arbitrary", "arbitrary"), vmem_limit_bytes=VMEM_LIMIT),
        name="in_proj",
    )(h, front, gain, w_main, w_dt_blk, w_ba_blk)


def _row_specs(rt, nt):
    def tile(width, col):
        return pl.BlockSpec((rt, width), lambda b, t: (b * nt + t, col))

    def halo(width, col):
        return pl.BlockSpec(
            (HALO, width), lambda b, t: (jnp.maximum((b * nt + t) * (rt // HALO) - 1, 0), col))

    def const(shape):
        return pl.BlockSpec(shape, lambda b, t: (0,) * len(shape), pipeline_mode=pl.Buffered(1))

    def inst(*tail):
        shape = (rt // CHUNK, DN_HEADS) + tail
        return pl.BlockSpec(shape, lambda b, t: (b * nt + t,) + (0,) * (len(shape) - 1))

    return tile, halo, const, inst


def _dn_prep_kernel(qkv_ref, halo_ref, small_ref, cw_ref, vec_ref, eg_ref, eb_ref, smat_ref,
                    qt_ref, rhs_ref, kd_ref, l_ref, a_ref, egl_ref, xe_ref, *, chunks):
    t = pl.program_id(1)
    rows = chunks * CHUNK
    _stage_rows(qkv_ref, halo_ref, xe_ref, t == 0)

    sm = small_ref[...]
    lane = _iota(sm.shape, 1)
    real = (t * rows + _iota(sm.shape, 0)) >= PAD_FRONT
    is_beta = (lane >= LANE_BETA) & (lane < LANE_BETA + DN_HEADS)
    is_a = (lane >= LANE_A) & (lane < LANE_A + DN_HEADS)
    g = -jnp.exp(vec_ref[1:2, :]) * _softplus(sm + vec_ref[0:1, :])
    g = jnp.where(real & is_a, g, 0.0)
    beta = jnp.where(real & is_beta, _sigmoid(sm), 0.0)
    gcum = _sel_dot(_chunk_tril_bf16(rows), g)
    ge = _dot_sel(gcum, eg_ref[...])
    be = _dot_sel(beta, eb_ref[...], terms=2)
    gl = _last_row_of_chunks(ge, chunks)
    eg = jnp.exp(ge)
    ekd = jnp.exp(gl - ge)

    li = _iota((CHUNK, CHUNK), 0)
    si = _iota((CHUNK, CHUNK), 1)
    causal = li >= si
    strict = li > si
    eye = _eye_bf16()
    cr = [slice(c * CHUNK, (c + 1) * CHUNK) for c in range(chunks)]
    egl = [jnp.exp(ge[cr[c].stop - 1:cr[c].stop, :]) for c in range(chunks)]
    gct = [_sel_dot_nt(eye, gcum[cr[c]]) for c in range(chunks)]

    def l2n(vh, scale):
        ss = jnp.sum(vh * vh, axis=-1, keepdims=True)
        return vh * (lax.rsqrt(ss + NORM_EPS) * scale)

    heads_per_block = CONV_COLS // DN_HEAD_DIM
    qkv = []
    for j, (cs, acc) in enumerate(_conv_blocks(xe_ref, smat_ref, cw_ref, 3 * D_MODEL, rows)):
        val = _silu(acc)
        for i in range(heads_per_block):
            vh = val[:, i * DN_HEAD_DIM:(i + 1) * DN_HEAD_DIM]
            if j < D_MODEL // CONV_COLS:
                vh = l2n(vh, DN_HEAD_DIM ** -0.5)
            elif j < 2 * D_MODEL // CONV_COLS:
                vh = l2n(vh, 1.0)
            qkv.append(vh)
    qs, ks, vs = qkv[:DN_HEADS], qkv[DN_HEADS:2 * DN_HEADS], qkv[2 * DN_HEADS:]

    for h in range(DN_HEADS):
        hs = slice(h * DN_HEAD_DIM, (h + 1) * DN_HEAD_DIM)
        bt = be[:, hs]
        qt_ref[:, hs] = (qs[h] * eg[:, hs]).astype(BF16)
        rhs_ref[:, 2 * h * DN_HEAD_DIM:(2 * h + 1) * DN_HEAD_DIM] = (vs[h] * bt).astype(BF16)
        rhs_ref[:, (2 * h + 1) * DN_HEAD_DIM:(2 * h + 2) * DN_HEAD_DIM] = (ks[h] * (bt * eg[:, hs])).astype(BF16)
        kd_ref[:, hs] = (ks[h] * ekd[:, hs]).astype(BF16)

    for c in range(chunks):
        for h in range(DN_HEADS):
            c0 = h * DN_HEAD_DIM
            kb = ks[h][cr[c]].astype(BF16)
            kq = _dot_nt(jnp.concatenate([kb, qs[h][cr[c]].astype(BF16)], axis=0), kb)
            gs = gct[c][LANE_A + h:LANE_A + h + 1, :]
            decay = jnp.where(causal, jnp.exp(jnp.minimum(ge[cr[c], c0:c0 + CHUNK] - gs, 0.0)), 0.0)
            l_ref[c, h] = jnp.where(strict, be[cr[c], c0:c0 + CHUNK] * kq[:CHUNK] * decay, 0.0)
            a_ref[c, h] = (kq[CHUNK:] * decay).astype(BF16)
            egl_ref[c, h:h + 1, :] = egl[c][:, c0:c0 + LANES]


def _dn_prep(main, small, conv_w, vec, eg, eb, smat, batch, nc, chunks):
    rows = main.shape[0]
    nt = nc // chunks
    rt = chunks * CHUNK
    tile, halo, const, inst = _row_specs(rt, nt)
    return pl.pallas_call(
        functools.partial(_dn_prep_kernel, chunks=chunks),
        grid=(batch, nt),
        in_specs=[
            tile(3 * D_MODEL, COL_QKV), halo(3 * D_MODEL, COL_QKV), tile(LANES, 0),
            const((CONV_TAPS, 3 * D_MODEL)), const((8, LANES)),
            const((LANES, D_MODEL)), const((LANES, D_MODEL)), const(smat.shape),
        ],
        out_specs=[
            tile(D_MODEL, 0), tile(2 * D_MODEL, 0), tile(D_MODEL, 0),
            inst(CHUNK, CHUNK), inst(CHUNK, CHUNK), inst(LANES),
        ],
        out_shape=[
            jax.ShapeDtypeStruct((rows, D_MODEL), BF16),
            jax.ShapeDtypeStruct((rows, 2 * D_MODEL), BF16),
            jax.ShapeDtypeStruct((rows, D_MODEL), BF16),
            jax.ShapeDtypeStruct((batch * nc, DN_HEADS, CHUNK, CHUNK), F32),
            jax.ShapeDtypeStruct((batch * nc, DN_HEADS, CHUNK, CHUNK), BF16),
            jax.ShapeDtypeStruct((batch * nc, DN_HEADS, LANES), F32),
        ],
        scratch_shapes=[pltpu.VMEM((HALO + rt, 3 * D_MODEL), BF16)],
        compiler_params=pltpu.CompilerParams(
            dimension_semantics=("arbitrary", "arbitrary"), vmem_limit_bytes=VMEM_LIMIT),
        name="dn_prep",
    )(main, main, small, conv_w, vec, eg, eb, smat)


def _dn_solve_kernel(lt_ref, tt_ref):
    sub = 8
    nblk = CHUNK // sub
    tt_ref[...] = jnp.zeros_like(tt_ref)
    col = _iota((sub, LANES), 0)

    for ib in range(nblk):
        def row_body(i, carry, ib=ib):
            acc = [(col + k * sub == i).astype(F32) for k in range(ib + 1)]
            for jb in range(ib + 1):
                blk = lt_ref[i, jb * sub:(jb + 1) * sub, :]
                for jj in range(sub):
                    j = jb * sub + jj
                    lij = blk[jj:jj + 1, :]
                    for k in range(jb + 1):
                        acc[k] = acc[k] - lij * tt_ref[j, k * sub:(k + 1) * sub, :]
            tt_ref[i, 0:(ib + 1) * sub, :] = jnp.concatenate(acc, axis=0)
            return carry

        lax.fori_loop(ib * sub, (ib + 1) * sub, row_body, 0)


def _dn_solve(lt):
    ni = lt.shape[-1]
    spec = pl.BlockSpec((CHUNK, CHUNK, LANES), lambda i: (0, 0, i))
    return pl.pallas_call(
        _dn_solve_kernel,
        grid=(pl.cdiv(ni, LANES),),
        in_specs=[spec],
        out_specs=spec,
        out_shape=jax.ShapeDtypeStruct(lt.shape, F32),
        compiler_params=pltpu.CompilerParams(dimension_semantics=("arbitrary",)),
        name="dn_solve",
    )(lt)


def _dn_scan_kernel(qt_ref, rhs_ref, kd_ref, t_ref, a_ref, egl_ref, z_ref, nw_ref, y_ref, s_ref,
                    *, chunks, nb):
    @pl.when(pl.program_id(1) == 0)
    def _():
        s_ref[...] = jnp.zeros_like(s_ref)

    nw = nw_ref[...]
    units = [(b, h) for b in range(nb) for h in range(DN_HEADS)]
    uw = [[_dot(t_ref[b, c, h],
                rhs_ref[b, c * CHUNK:(c + 1) * CHUNK, 2 * h * DN_HEAD_DIM:(2 * h + 2) * DN_HEAD_DIM])
           for (b, h) in units] for c in range(chunks)]
    s = [s_ref[b * DN_HEADS + h] for (b, h) in units]
    for c in range(chunks):
        rc = slice(c * CHUNK, (c + 1) * CHUNK)
        ws = []
        for i, (b, h) in enumerate(units):
            hs = slice(h * DN_HEAD_DIM, (h + 1) * DN_HEAD_DIM)
            wq = jnp.concatenate([uw[c][i][:, DN_HEAD_DIM:].astype(BF16), qt_ref[b, rc, hs]], axis=0)
            ws.append(_dot(wq, s[i].astype(BF16)))
        for i, (b, h) in enumerate(units):
            hs = slice(h * DN_HEAD_DIM, (h + 1) * DN_HEAD_DIM)
            vn = (uw[c][i][:, :DN_HEAD_DIM] - ws[i][:CHUNK]).astype(BF16)
            o = ws[i][CHUNK:] + _dot(a_ref[b, c, h], vn)
            s[i] = s[i] * egl_ref[b, c, h:h + 1, :] + _dot_tn(kd_ref[b, rc, hs], vn)
            ms = jnp.mean(o * o, axis=-1, keepdims=True)
            on = o * lax.rsqrt(ms + NORM_EPS) * nw
            y_ref[b, rc, hs] = (on * _silu(z_ref[b, rc, hs].astype(F32))).astype(BF16)
    for i, (b, h) in enumerate(units):
        s_ref[b * DN_HEADS + h] = s[i]


def _dn_scan(qt, rhs, kd, tmat, amat, egl, main, norm_w, batch, nc, chunks):
    tp = nc * CHUNK
    nt = nc // chunks
    rt = chunks * CHUNK
    nb = 2 if batch % 2 == 0 else 1

    def seq_tile(width, col):
        return pl.BlockSpec((nb, rt, width), lambda p, t: (p, t, col))

    def inst(*tail):
        shape = (nb, chunks, DN_HEADS) + tail
        return pl.BlockSpec(shape, lambda p, t: (p, t) + (0,) * (len(shape) - 2))

    def by_seq(a):
        return a.reshape((batch, a.shape[0] // batch) + a.shape[1:])

    y = pl.pallas_call(
        functools.partial(_dn_scan_kernel, chunks=chunks, nb=nb),
        grid=(batch // nb, nt),
        in_specs=[
            seq_tile(D_MODEL, 0), seq_tile(2 * D_MODEL, 0), seq_tile(D_MODEL, 0), inst(CHUNK, CHUNK),
            inst(CHUNK, CHUNK), inst(LANES), seq_tile(D_MODEL, COL_DN_Z),
            pl.BlockSpec((1, DN_HEAD_DIM), lambda p, t: (0, 0), pipeline_mode=pl.Buffered(1)),
        ],
        out_specs=seq_tile(D_MODEL, 0),
        out_shape=jax.ShapeDtypeStruct((batch, tp, D_MODEL), BF16),
        scratch_shapes=[pltpu.VMEM((nb * DN_HEADS, DN_HEAD_DIM, DN_HEAD_DIM), F32)],
        compiler_params=pltpu.CompilerParams(
            dimension_semantics=("arbitrary", "arbitrary"), vmem_limit_bytes=VMEM_LIMIT),
        name="dn_scan",
    )(by_seq(qt), by_seq(rhs), by_seq(kd), by_seq(tmat), by_seq(amat), by_seq(egl), by_seq(main), norm_w)
    return y.reshape(batch * tp, D_MODEL)


def _ssd_kernel(xbc_ref, halo_ref, z_ref, small_ref, cw_ref, cb_ref, vec_ref, e16_ref, psel_ref,
                de_ref, nw_ref, smat_ref, y_ref, xe_ref, h_ref, *, chunks, nb):
    t = pl.program_id(1)
    rows = chunks * CHUNK
    nch = nb * chunks

    @pl.when(t == 0)
    def _():
        h_ref[...] = jnp.zeros_like(h_ref)

    for b in range(nb):
        _stage_rows(xbc_ref.at[b], halo_ref.at[b], xe_ref.at[b], t == 0)

    sm = jnp.concatenate([small_ref[b] for b in range(nb)], axis=0)
    lane = _iota((rows, LANES), 1)
    real = ((t * rows + _iota((rows, LANES), 0)) >= PAD_FRONT) & (lane < SSD_HEADS)
    dt = jnp.where(jnp.concatenate([real] * nb, axis=0), _softplus(sm + vec_ref[0:1, :]), 0.0)
    da = dt * (-jnp.exp(vec_ref[1:2, :]))
    tril = _chunk_tril_bf16(rows)
    acum = jnp.concatenate([_sel_dot(tril, da[b * rows:(b + 1) * rows]) for b in range(nb)], axis=0)
    ae = _dot_sel(acum, e16_ref[...])
    dte = _dot_sel(dt, e16_ref[...], terms=2)
    alast = _last_row_of_chunks(ae, nch)

    conv = [[_silu(acc + cb_ref[:, cs])
             for cs, acc in _conv_blocks(xe_ref.at[b], smat_ref, cw_ref, SSD_XBC, rows)] for b in range(nb)]
    nx = D_MODEL // CONV_COLS
    xs = jnp.concatenate([jnp.concatenate(cb[:nx], axis=1) for cb in conv], axis=0)
    bc = jnp.concatenate([jnp.concatenate(cb[nx:], axis=1) for cb in conv], axis=0).astype(BF16)
    xc = xs * dte
    xcb = xc.astype(BF16)
    xdec = (xc * jnp.exp(alast - ae)).astype(BF16)
    eae = jnp.exp(ae)
    ealast = [jnp.exp(ae[(c + 1) * CHUNK - 1:(c + 1) * CHUNK, :]) for c in range(nch)]

    half = SSD_HEADS // SSD_GROUPS * SSD_HEAD_DIM
    pairs = SSD_HEADS // SSD_GROUPS // 2
    cr = [slice(c * CHUNK, (c + 1) * CHUNK) for c in range(nch)]
    gsl = [slice(g * half, (g + 1) * half) for g in range(SSD_GROUPS)]
    bgs = [[bc[cr[c], g * SSD_STATE:(g + 1) * SSD_STATE] for g in range(SSD_GROUPS)] for c in range(nch)]
    cgs = [[bc[cr[c], (SSD_GROUPS + g) * SSD_STATE:(SSD_GROUPS + g + 1) * SSD_STATE]
            for g in range(SSD_GROUPS)] for c in range(nch)]

    states = [[_dot_tn(bgs[c][g], xdec[cr[c], gsl[g]]) for g in range(SSD_GROUPS)] for c in range(nch)]
    yoff = [[None] * SSD_GROUPS for _ in range(nch)]
    for b in range(nb):
        for g in range(SSD_GROUPS):
            hg = h_ref[b * SSD_GROUPS + g]
            for c in range(b * chunks, (b + 1) * chunks):
                yoff[c][g] = _dot(cgs[c][g], hg.astype(BF16)) * eae[cr[c], gsl[g]]
                hg = hg * ealast[c][:, gsl[g]] + states[c][g]
            h_ref[b * SSD_GROUPS + g] = hg

    psel = psel_ref[...]
    prow = _iota((2 * CHUNK, LANES), 0)
    plane = _iota((2 * CHUNK, LANES), 1)
    keep = (prow < CHUNK) == ((plane & 1) == 0)
    bd_mask = (prow < CHUNK) == (plane < CHUNK)
    causal2 = _iota((CHUNK, LANES), 0) >= (_iota((CHUNK, LANES), 1) & (CHUNK - 1))
    cbcb = [[_dot_nt(cgs[c][g], jnp.concatenate([bgs[c][g], bgs[c][g]], axis=0))
             for g in range(SSD_GROUPS)] for c in range(nch)]
    rowpair = []
    for c in range(nch):
        ac = acum[cr[c]]
        a2 = jnp.where(keep, jnp.concatenate([ac, ac], axis=0), 0.0)
        rowpair.append(_sel_dot_nt(psel, a2))
    nw = nw_ref[...]
    de = de_ref[...]
    for c in range(nch):
        b, lc = c // chunks, slice((c % chunks) * CHUNK, (c % chunks + 1) * CHUNK)
        ys = []
        for g in range(SSD_GROUPS):
            for p in range(pairs):
                j = g * pairs + p
                js = slice(j * LANES, (j + 1) * LANES)
                diff = ae[cr[c], js] - rowpair[c][j:j + 1, :]
                m = jnp.where(causal2, cbcb[c][g] * jnp.exp(jnp.minimum(diff, 0.0)), 0.0).astype(BF16)
                xp = xcb[cr[c], js]
                bd = jnp.where(bd_mask, jnp.concatenate([xp, xp], axis=0), jnp.zeros((), BF16))
                ys.append(_dot(m, bd) + yoff[c][g][:, p * LANES:(p + 1) * LANES])
        y = jnp.concatenate(ys, axis=1) + de * xs[cr[c]]
        y = y * _silu(z_ref[b, lc, :].astype(F32))
        for g in range(SSD_GROUPS):
            yg = y[:, gsl[g]]
            ms = jnp.mean(yg * yg, axis=-1, keepdims=True)
            y_ref[b, lc, gsl[g]] = (yg * lax.rsqrt(ms + NORM_EPS) * nw[:, gsl[g]]).astype(BF16)


def _ssd(main, small, conv_w, conv_b, vec, e16, psel, d_exp, norm_w, smat, batch, nc, chunks):
    tp = nc * CHUNK
    nt = nc // chunks
    rt = chunks * CHUNK
    nb = 2 if batch % 2 == 0 else 1
    _, _, const, _ = _row_specs(rt, nt)

    def seq_tile(width, col):
        return pl.BlockSpec((nb, rt, width), lambda p, t: (p, t, col))

    def seq_halo(width, col):
        return pl.BlockSpec((nb, HALO, width), lambda p, t: (p, jnp.maximum(t * (rt // HALO) - 1, 0), col))

    main3 = main.reshape(batch, tp, main.shape[1])
    y = pl.pallas_call(
        functools.partial(_ssd_kernel, chunks=chunks, nb=nb),
        grid=(batch // nb, nt),
        in_specs=[
            seq_tile(SSD_XBC, COL_XBC), seq_halo(SSD_XBC, COL_XBC), seq_tile(D_MODEL, COL_SSD_Z),
            seq_tile(LANES, 0),
            const((CONV_TAPS, SSD_XBC)), const((1, SSD_XBC)), const((8, LANES)),
            const((LANES, D_MODEL)), const((8, LANES)), const((1, D_MODEL)), const((1, D_MODEL)),
            const(smat.shape),
        ],
        out_specs=seq_tile(D_MODEL, 0),
        out_shape=jax.ShapeDtypeStruct((batch, tp, D_MODEL), BF16),
        scratch_shapes=[
            pltpu.VMEM((nb, HALO + rt, SSD_XBC), BF16),
            pltpu.VMEM((nb * SSD_GROUPS, SSD_STATE, D_MODEL // SSD_GROUPS), F32),
        ],
        compiler_params=pltpu.CompilerParams(
            dimension_semantics=("arbitrary", "arbitrary"), vmem_limit_bytes=VMEM_LIMIT),
        name="ssd",
    )(main3, main3, main3, small.reshape(batch, tp, LANES), conv_w, conv_b, vec, e16, psel, d_exp, norm_w, smat)
    return y.reshape(batch * tp, D_MODEL)


def _out_kernel(yssd_ref, ydn_ref, b_ref, c_ref, chalo_ref, h_ref, hhalo_ref, g_ref, gates_ref,
                x_ref, front_ref, scw_ref, wb_ref, wo_ref, np_ref, o_ref, *, row0, x_unpadded):
    t = pl.program_id(1)
    rows = x_ref.shape[0]
    sub = 8
    u = c_ref[...].astype(F32) * h_ref[...].astype(F32)
    uh = chalo_ref[...].astype(F32) * hhalo_ref[...].astype(F32)
    if row0 == 0:
        uh = jnp.where(t == 0, 0.0, uh)
    w0, w1, w2 = scw_ref[0:1, :], scw_ref[1:2, :], scw_ref[2:3, :]
    conv = w2 * u + w1 * pltpu.roll(u, 1, 0) + w0 * pltpu.roll(u, 2, 0)
    head = jnp.concatenate([uh, u[0:sub]], axis=0)
    fix = (w2 * head[HALO:HALO + sub] + w1 * head[HALO - 1:HALO - 1 + sub]
           + w0 * head[HALO - 2:HALO - 2 + sub])
    conv = jnp.concatenate([fix, conv[sub:]], axis=0)
    ysc = (b_ref[...].astype(F32) * conv * _silu(g_ref[...].astype(F32))).astype(BF16)

    merged = None
    for n, y in enumerate((yssd_ref[...], ysc, ydn_ref[...])):
        gate = _sigmoid(gates_ref[:, n * D_MODEL:(n + 1) * D_MODEL].astype(F32))
        term = gate * _dot(y, wb_ref[n])
        merged = term if merged is None else merged + term
    out = _dot(merged.astype(BF16), wo_ref[...])
    ms = jnp.mean(out * out, axis=-1, keepdims=True)
    resid = x_ref[...]
    if x_unpadded and row0 == 0:
        resid = _with_front_rows(resid, front_ref, t == 0)
    res = resid + out * lax.rsqrt(ms + NORM_EPS) * np_ref[...]
    if row0 < PAD_FRONT:
        res = jnp.where((row0 + t * rows + _iota(res.shape, 0)) >= PAD_FRONT, res, 0.0)
    o_ref[...] = res


def _out(yssd, ydn, main, x, front, sc_w, w_branch, w_out, norm_post, batch, tp, rt, row0, x_unpadded):
    nt = (tp - row0) // rt
    rows = batch * nt * rt
    _, _, const, _ = _row_specs(rt, nt)
    if row0 == 0:
        tile, halo, _, _ = _row_specs(rt, nt)
    else:
        def tile(width, col):
            return pl.BlockSpec(
                (pl.Element(rt), pl.Element(width)),
                lambda b, t: ((b * (tp // HALO) + row0 // HALO + t * (rt // HALO)) * HALO, col * width))

        def halo(width, col):
            return pl.BlockSpec(
                (pl.Element(HALO), pl.Element(width)),
                lambda b, t: ((b * (tp // HALO) + row0 // HALO + t * (rt // HALO) - 1) * HALO, col * width))

    out_spec = pl.BlockSpec((rt, D_MODEL), lambda b, t: (b * nt + t, 0))
    if not x_unpadded:
        x_spec = tile(D_MODEL, 0)
    elif row0 == 0:
        x_spec = _unpadded_row_spec(rt, nt, tp - CHUNK)(lambda b, t: (b, t))
    else:
        x_spec = out_spec
    return pl.pallas_call(
        functools.partial(_out_kernel, row0=row0, x_unpadded=x_unpadded),
        grid=(batch, nt),
        in_specs=[
            tile(D_MODEL, 0), tile(D_MODEL, 0),
            tile(D_MODEL, COL_SC_B), tile(D_MODEL, COL_SC_C), halo(D_MODEL, COL_SC_C),
            tile(D_MODEL, COL_SC_H), halo(D_MODEL, COL_SC_H), tile(D_MODEL, COL_SC_G),
            tile(3 * D_MODEL, COL_GATES), x_spec, const((CHUNK, D_MODEL)),
            const((3, D_MODEL)), const((3, D_MODEL, D_MODEL)), const((D_MODEL, D_MODEL)),
            const((1, D_MODEL)),
        ],
        out_specs=out_spec,
        out_shape=jax.ShapeDtypeStruct((rows, D_MODEL), F32),
        compiler_params=pltpu.CompilerParams(
            dimension_semantics=("arbitrary", "arbitrary"), vmem_limit_bytes=VMEM_LIMIT),
        name="out_proj",
    )(yssd, ydn, main, main, main, main, main, main, main, x, front, sc_w, w_branch, w_out, norm_post)


def _lane_vec(rows):
    padded = [jnp.pad(v.astype(F32), (off, LANES - off - v.shape[0])) for off, v in rows]
    padded += [jnp.zeros((LANES,), F32)] * (8 - len(padded))
    return jnp.stack(padded)


def _expand_matrix(lane0, heads, width):
    r = jnp.arange(LANES)[:, None]
    c = jnp.arange(heads * width)[None, :]
    return (r == lane0 + c // width).astype(BF16)


def _shift_matrix(rows):
    r = jnp.arange((CONV_TAPS - 1) * rows)[:, None]
    c = jnp.arange(HALO + rows)[None, :]
    return (c == HALO - (CONV_TAPS - 1) + r % rows + r // rows).astype(BF16)


def _split_w_in(w):
    gates = w[:, IN_OFFS[11]:IN_OFFS[12]]
    qkv_dnz = w[:, IN_OFFS[7]:IN_OFFS[9]]
    sc = w[:, IN_OFFS[3]:IN_OFFS[7]]
    ssdz_xbc = w[:, IN_OFFS[0]:IN_OFFS[2]]
    return jnp.concatenate([p.astype(BF16) for p in (gates, qkv_dnz, sc, ssdz_xbc)], axis=1)


def kernel(x, meta_tokens, norm_pre, norm_post, w_in, ssd_conv_w, ssd_conv_b, ssd_dt_bias, ssd_a_log,
           ssd_d, ssd_norm, sc_conv_w, dn_conv_w, dn_dt_bias, dn_a_log, dn_norm, w_branch, w_out):
    batch, seq, d = x.shape
    assert d == D_MODEL and seq % CHUNK == 0
    tp = CHUNK + seq
    nc = tp // CHUNK
    chunks = 3 if nc % 3 == 0 else 1
    rows = batch * tp
    tm = tp // 2 if tp % 32 == 0 else tp
    rt_out = tp // 4 if tp % (4 * HALO) == 0 else tp
    rt_last = seq // 4 if seq % (4 * CHUNK) == 0 else CHUNK
    depth = w_in.shape[0]

    front = jnp.concatenate([jnp.zeros((PAD_FRONT, d), x.dtype), meta_tokens.astype(x.dtype)], axis=0)
    h = x.reshape(batch * seq, d)

    e16 = _expand_matrix(LANE_DT, SSD_HEADS, SSD_HEAD_DIM)
    eg = _expand_matrix(LANE_A, DN_HEADS, DN_HEAD_DIM)
    eb = _expand_matrix(LANE_BETA, DN_HEADS, DN_HEAD_DIM)
    smat = _shift_matrix(chunks * CHUNK)
    pj = jnp.arange(8)[:, None]
    pc = jnp.arange(LANES)[None, :]
    psel = ((pc // 2 == pj) & (pc < SSD_HEADS)).astype(BF16)

    for i in range(depth):
        unpadded = i == 0
        dt0 = OFF_DT // LANES * LANES
        ba0 = OFF_BETA // LANES * LANES
        main, small = _in_proj(h, front, norm_pre[i][None, :], _split_w_in(w_in[i]),
                               w_in[i, :, dt0:dt0 + LANES], w_in[i, :, ba0:ba0 + LANES], tm, tp, unpadded)

        dn_vec = _lane_vec([(LANE_A, dn_dt_bias[i]), (LANE_A, dn_a_log[i])])
        qt, rhs, kd, lmat, amat, egl = _dn_prep(
            main, small, dn_conv_w[i], dn_vec, eg, eb, smat, batch, nc, chunks)
        ni = batch * nc * DN_HEADS
        lt = jnp.transpose(lmat.reshape(ni, CHUNK, CHUNK), (1, 2, 0))
        if ni < LANES:
            lt = jnp.pad(lt, ((0, 0), (0, 0), (0, LANES - ni)))

        ssd_vec = _lane_vec([(LANE_DT, ssd_dt_bias[i]), (LANE_DT, ssd_a_log[i])])
        d_exp = jnp.repeat(ssd_d[i].astype(F32), SSD_HEAD_DIM)[None, :]
        yssd = _ssd(main, small, ssd_conv_w[i], ssd_conv_b[i][None, :], ssd_vec, e16, psel, d_exp,
                    ssd_norm[i][None, :], smat, batch, nc, chunks)

        tmat = jnp.transpose(_dn_solve(lt)[:, :, :ni], (2, 0, 1)).astype(BF16).reshape(
            batch * nc, DN_HEADS, CHUNK, CHUNK)
        ydn = _dn_scan(qt, rhs, kd, tmat, amat, egl, main, dn_norm[i][None, :], batch, nc, chunks)

        last = i == depth - 1
        h = _out(yssd, ydn, main, h, front, sc_conv_w[i], w_branch[i].astype(BF16), w_out[i].astype(BF16),
                 norm_post[i][None, :], batch, tp, rt_last if last else rt_out, CHUNK if last else 0,
                 unpadded)

    return h.reshape(batch, seq, d)
```

```python
import functools

import jax
import jax.numpy as jnp
from jax import lax
from jax.experimental import pallas as pl
from jax.experimental.pallas import tpu as pltpu

F32 = jnp.float32
BF16 = jnp.bfloat16

D_MODEL = 1024
N_META = 16
CHUNK = 64
CHUNK_SHIFT = 6
PAD_FRONT = CHUNK - N_META
NORM_EPS = 1e-6
SSD_HEADS = 16
SSD_HEAD_DIM = 64
SSD_GROUPS = 2
SSD_STATE = 128
SSD_XBC = 1536
DN_HEADS = 8
DN_HEAD_DIM = 128
CONV_TAPS = 4
LANES = 128
HALO = 16
CONV_COLS = 256
PROJ_TN = 2304
VMEM_LIMIT = 52 * 1024 * 1024

MAIN_WIDTH = 13824
COL_GATES = 0
COL_QKV = 1
COL_DN_Z = 6
COL_SC_B = 7
COL_SC_C = 8
COL_SC_H = 9
COL_SC_G = 10
COL_SSD_Z = 11
COL_XBC = 8
IN_SIZES = (1024, 1536, 16, 1024, 1024, 1024, 1024, 3072, 1024, 8, 8, 3072)
IN_OFFS = tuple(sum(IN_SIZES[:i]) for i in range(len(IN_SIZES) + 1))
OFF_DT, OFF_BETA, OFF_A = IN_OFFS[2], IN_OFFS[9], IN_OFFS[10]
LANE_DT = OFF_DT % LANES
LANE_BETA = OFF_BETA % LANES
LANE_A = OFF_A % LANES
assert OFF_BETA // LANES == OFF_A // LANES and LANE_DT + SSD_HEADS <= LANE_BETA < LANE_A


def _dot(a, b):
    return jnp.dot(a, b, preferred_element_type=F32)


def _dot_nt(a, b):
    return lax.dot_general(a, b, (((1,), (1,)), ((), ())), preferred_element_type=F32)


def _dot_tn(a, b):
    return lax.dot_general(a, b, (((0,), (0,)), ((), ())), preferred_element_type=F32)


def _split3(x):
    hi = x.astype(BF16)
    r1 = x - hi.astype(F32)
    mid = r1.astype(BF16)
    lo = (r1 - mid.astype(F32)).astype(BF16)
    return hi, mid, lo


def _expand_lanes(x, lane0, heads, width):
    rows = x.shape[0]
    cols = [jnp.broadcast_to(x[:, lane0 + h:lane0 + h + 1], (rows, LANES)) for h in range(heads)]
    if width == LANES:
        return jnp.concatenate(cols, axis=1)
    low = _iota((rows, LANES), 1) < width
    return jnp.concatenate([jnp.where(low, cols[h], cols[h + 1]) for h in range(0, heads, 2)], axis=1)


def _sel_dot(sel, x):
    hi, mid, lo = _split3(x)
    return _dot(sel, hi) + _dot(sel, mid) + _dot(sel, lo)


def _sel_dot_nt(sel, x):
    hi, mid, lo = _split3(x)
    return _dot_nt(sel, hi) + _dot_nt(sel, mid) + _dot_nt(sel, lo)


def _sigmoid(x):
    return 1.0 / (1.0 + jnp.exp(-x))


def _silu(x):
    return x * _sigmoid(x)


def _softplus(x):
    return jnp.maximum(x, 0.0) + jnp.log(1.0 + jnp.exp(-jnp.abs(x)))


def _iota(shape, dim):
    return lax.broadcasted_iota(jnp.int32, shape, dim)


def _chunk_tril_bf16(rows):
    r = _iota((rows, rows), 0)
    c = _iota((rows, rows), 1)
    same = lax.shift_right_logical(r, CHUNK_SHIFT) == lax.shift_right_logical(c, CHUNK_SHIFT)
    return ((r >= c) & same).astype(BF16)


def _eye_bf16(n=LANES):
    return (_iota((n, n), 0) == _iota((n, n), 1)).astype(BF16)


def _last_row_of_chunks(x, chunks):
    return jnp.concatenate(
        [jnp.broadcast_to(x[(c + 1) * CHUNK - 1:(c + 1) * CHUNK, :], (CHUNK, x.shape[1]))
         for c in range(chunks)], axis=0)


def _stage_rows(x_ref, halo_ref, xe_ref, first):
    rows = x_ref.shape[0]
    halo = halo_ref[...]
    xe_ref[0:HALO, :] = jnp.where(first, jnp.zeros_like(halo), halo)
    xe_ref[HALO:HALO + rows, :] = x_ref[...]


def _conv_blocks(xe_ref, smat_ref, w_ref, width, rows):
    slices = [slice(j * CONV_COLS, (j + 1) * CONV_COLS) for j in range(width // CONV_COLS)]
    sh_next = _dot(smat_ref[...], xe_ref[:, slices[0]])
    for j, cs in enumerate(slices):
        sh = sh_next
        if j + 1 < len(slices):
            sh_next = _dot(smat_ref[...], xe_ref[:, slices[j + 1]])
        acc = w_ref[CONV_TAPS - 1:CONV_TAPS, cs] * xe_ref[HALO:HALO + rows, cs].astype(F32)
        for k in range(CONV_TAPS - 1):
            acc = acc + w_ref[k:k + 1, cs] * sh[k * rows:(k + 1) * rows]
        yield cs, acc


def _with_front_rows(x, front_ref, is_first):
    shifted = jnp.concatenate([front_ref[...], x[:x.shape[0] - CHUNK]], axis=0)
    return jnp.where(is_first, shifted, x)


def _unpadded_row_spec(rt, tiles_per_batch, seq):
    def index(b, t):
        row = b * (seq // HALO) + jnp.maximum(t * (rt // HALO) - CHUNK // HALO, 0)
        return row * HALO, 0

    return lambda grid_to_bt: pl.BlockSpec(
        (pl.Element(rt), pl.Element(D_MODEL)), lambda *g: index(*grid_to_bt(*g)))


def _in_proj_kernel(x_ref, front_ref, gain_ref, w_ref, wdt_ref, wba_ref, main_ref, small_ref, xn_ref,
                    *, unpadded, tiles_per_batch):
    @pl.when(pl.program_id(1) == 0)
    def _():
        x = x_ref[...]
        if unpadded:
            x = _with_front_rows(x, front_ref, pl.program_id(0) % tiles_per_batch == 0)
        ms = jnp.mean(x * x, axis=-1, keepdims=True)
        xn = (x * lax.rsqrt(ms + NORM_EPS) * gain_ref[...]).astype(BF16)
        xn_ref[...] = xn
        lane = _iota(wdt_ref.shape, 1)
        ws = jnp.where(lane < LANE_DT + SSD_HEADS, wdt_ref[...],
                       jnp.where((lane >= LANE_BETA) & (lane < LANE_A + DN_HEADS), wba_ref[...], 0.0))
        small_ref[...] = _dot(xn, ws.astype(BF16))

    main_ref[...] = _dot(xn_ref[...], w_ref[...]).astype(BF16)


def _in_proj(h, front, gain, w_main, w_dt_blk, w_ba_blk, tm, tp, unpadded):
    tiles_per_batch = tp // tm
    rows = h.shape[0] // (tp - CHUNK) * tp if unpadded else h.shape[0]
    grid = (rows // tm, MAIN_WIDTH // PROJ_TN)
    if unpadded:
        x_spec = _unpadded_row_spec(tm, tiles_per_batch, tp - CHUNK)(
            lambda i, j: (i // tiles_per_batch, i % tiles_per_batch))
    else:
        x_spec = pl.BlockSpec((tm, D_MODEL), lambda i, j: (i, 0))
    return pl.pallas_call(
        functools.partial(_in_proj_kernel, unpadded=unpadded, tiles_per_batch=tiles_per_batch),
        grid=grid,
        in_specs=[
            x_spec,
            pl.BlockSpec((CHUNK, D_MODEL), lambda i, j: (0, 0)),
            pl.BlockSpec((1, D_MODEL), lambda i, j: (0, 0)),
            pl.BlockSpec((D_MODEL, PROJ_TN), lambda i, j: (0, j)),
            pl.BlockSpec((D_MODEL, LANES), lambda i, j: (0, 0)),
            pl.BlockSpec((D_MODEL, LANES), lambda i, j: (0, 0)),
        ],
        out_specs=[
            pl.BlockSpec((tm, PROJ_TN), lambda i, j: (i, j)),
            pl.BlockSpec((tm, LANES), lambda i, j: (i, 0)),
        ],
        out_shape=[
            jax.ShapeDtypeStruct((rows, MAIN_WIDTH), BF16),
            jax.ShapeDtypeStruct((rows, LANES), F32),
        ],
        scratch_shapes=[pltpu.VMEM((tm, D_MODEL), BF16)],
        compiler_params=pltpu.CompilerParams(
            dimension_semantics=("arbitrary", "arbitrary"), vmem_limit_bytes=VMEM_LIMIT),
        name="in_proj",
    )(h, front, gain, w_main, w_dt_blk, w_ba_blk)


def _row_specs(rt, nt):
    def tile(width, col):
        return pl.BlockSpec((rt, width), lambda b, t: (b * nt + t, col))

    def halo(width, col):
        return pl.BlockSpec(
            (HALO, width), lambda b, t: (jnp.maximum((b * nt + t) * (rt // HALO) - 1, 0), col))

    def const(shape):
        return pl.BlockSpec(shape, lambda b, t: (0,) * len(shape), pipeline_mode=pl.Buffered(1))

    def inst(*tail):
        shape = (rt // CHUNK, DN_HEADS) + tail
        return pl.BlockSpec(shape, lambda b, t: (b * nt + t,) + (0,) * (len(shape) - 1))

    return tile, halo, const, inst


def _dn_prep_kernel(qkv_ref, halo_ref, small_ref, cw_ref, vec_ref, smat_ref,
                    qt_ref, rhs_ref, kd_ref, l_ref, a_ref, egl_ref, xe_ref, *, chunks):
    t = pl.program_id(1)
    rows = chunks * CHUNK
    _stage_rows(qkv_ref, halo_ref, xe_ref, t == 0)

    sm = small_ref[...]
    lane = _iota(sm.shape, 1)
    real = (t * rows + _iota(sm.shape, 0)) >= PAD_FRONT
    is_beta = (lane >= LANE_BETA) & (lane < LANE_BETA + DN_HEADS)
    is_a = (lane >= LANE_A) & (lane < LANE_A + DN_HEADS)
    g = -jnp.exp(vec_ref[1:2, :]) * _softplus(sm + vec_ref[0:1, :])
    g = jnp.where(real & is_a, g, 0.0)
    beta = jnp.where(real & is_beta, _sigmoid(sm), 0.0)
    gcum = _sel_dot(_chunk_tril_bf16(rows), g)
    ge = _expand_lanes(gcum, LANE_A, DN_HEADS, DN_HEAD_DIM)
    be = _expand_lanes(beta, LANE_BETA, DN_HEADS, DN_HEAD_DIM)
    gl = _last_row_of_chunks(ge, chunks)
    eg = jnp.exp(ge)
    ekd = jnp.exp(gl - ge)

    li = _iota((CHUNK, CHUNK), 0)
    si = _iota((CHUNK, CHUNK), 1)
    causal = li >= si
    strict = li > si
    eye = _eye_bf16()
    cr = [slice(c * CHUNK, (c + 1) * CHUNK) for c in range(chunks)]
    egl = [jnp.exp(ge[cr[c].stop - 1:cr[c].stop, :]) for c in range(chunks)]
    gct = [_sel_dot_nt(eye, gcum[cr[c]]) for c in range(chunks)]

    def l2n(vh, scale):
        ss = jnp.sum(vh * vh, axis=-1, keepdims=True)
        return vh * (lax.rsqrt(ss + NORM_EPS) * scale)

    heads_per_block = CONV_COLS // DN_HEAD_DIM
    qkv = []
    for j, (cs, acc) in enumerate(_conv_blocks(xe_ref, smat_ref, cw_ref, 3 * D_MODEL, rows)):
        val = _silu(acc)
        for i in range(heads_per_block):
            vh = val[:, i * DN_HEAD_DIM:(i + 1) * DN_HEAD_DIM]
            if j < D_MODEL // CONV_COLS:
                vh = l2n(vh, DN_HEAD_DIM ** -0.5)
            elif j < 2 * D_MODEL // CONV_COLS:
                vh = l2n(vh, 1.0)
            qkv.append(vh)
    qs, ks, vs = qkv[:DN_HEADS], qkv[DN_HEADS:2 * DN_HEADS], qkv[2 * DN_HEADS:]

    for h in range(DN_HEADS):
        hs = slice(h * DN_HEAD_DIM, (h + 1) * DN_HEAD_DIM)
        bt = be[:, hs]
        qt_ref[:, hs] = (qs[h] * eg[:, hs]).astype(BF16)
        rhs_ref[:, 2 * h * DN_HEAD_DIM:(2 * h + 1) * DN_HEAD_DIM] = (vs[h] * bt).astype(BF16)
        rhs_ref[:, (2 * h + 1) * DN_HEAD_DIM:(2 * h + 2) * DN_HEAD_DIM] = (ks[h] * (bt * eg[:, hs])).astype(BF16)
        kd_ref[:, hs] = (ks[h] * ekd[:, hs]).astype(BF16)

    for c in range(chunks):
        for h in range(DN_HEADS):
            c0 = h * DN_HEAD_DIM
            kb = ks[h][cr[c]].astype(BF16)
            kq = _dot_nt(jnp.concatenate([kb, qs[h][cr[c]].astype(BF16)], axis=0), kb)
            gs = gct[c][LANE_A + h:LANE_A + h + 1, :]
            decay = jnp.where(causal, jnp.exp(jnp.minimum(ge[cr[c], c0:c0 + CHUNK] - gs, 0.0)), 0.0)
            l_ref[c, h] = jnp.where(strict, be[cr[c], c0:c0 + CHUNK] * kq[:CHUNK] * decay, 0.0)
            a_ref[c, h] = (kq[CHUNK:] * decay).astype(BF16)
            egl_ref[c, h:h + 1, :] = egl[c][:, c0:c0 + LANES]


def _dn_prep(main, small, conv_w, vec, smat, batch, nc, chunks):
    rows = main.shape[0]
    nt = nc // chunks
    rt = chunks * CHUNK
    tile, halo, const, inst = _row_specs(rt, nt)
    return pl.pallas_call(
        functools.partial(_dn_prep_kernel, chunks=chunks),
        grid=(batch, nt),
        in_specs=[
            tile(3 * D_MODEL, COL_QKV), halo(3 * D_MODEL, COL_QKV), tile(LANES, 0),
            const((CONV_TAPS, 3 * D_MODEL)), const((8, LANES)), const(smat.shape),
        ],
        out_specs=[
            tile(D_MODEL, 0), tile(2 * D_MODEL, 0), tile(D_MODEL, 0),
            inst(CHUNK, CHUNK), inst(CHUNK, CHUNK), inst(LANES),
        ],
        out_shape=[
            jax.ShapeDtypeStruct((rows, D_MODEL), BF16),
            jax.ShapeDtypeStruct((rows, 2 * D_MODEL), BF16),
            jax.ShapeDtypeStruct((rows, D_MODEL), BF16),
            jax.ShapeDtypeStruct((batch * nc, DN_HEADS, CHUNK, CHUNK), F32),
            jax.ShapeDtypeStruct((batch * nc, DN_HEADS, CHUNK, CHUNK), BF16),
            jax.ShapeDtypeStruct((batch * nc, DN_HEADS, LANES), F32),
        ],
        scratch_shapes=[pltpu.VMEM((HALO + rt, 3 * D_MODEL), BF16)],
        compiler_params=pltpu.CompilerParams(
            dimension_semantics=("arbitrary", "arbitrary"), vmem_limit_bytes=VMEM_LIMIT),
        name="dn_prep",
    )(main, main, small, conv_w, vec, smat)


def _dn_solve_kernel(lt_ref, tt_ref):
    sub = 8
    nblk = CHUNK // sub
    tt_ref[...] = jnp.zeros_like(tt_ref)
    col = _iota((sub, LANES), 0)

    for ib in range(nblk):
        def row_body(i, carry, ib=ib):
            acc = [(col + k * sub == i).astype(F32) for k in range(ib + 1)]
            for jb in range(ib + 1):
                blk = lt_ref[i, jb * sub:(jb + 1) * sub, :]
                for jj in range(sub):
                    j = jb * sub + jj
                    lij = blk[jj:jj + 1, :]
                    for k in range(jb + 1):
                        acc[k] = acc[k] - lij * tt_ref[j, k * sub:(k + 1) * sub, :]
            tt_ref[i, 0:(ib + 1) * sub, :] = jnp.concatenate(acc, axis=0)
            return carry

        lax.fori_loop(ib * sub, (ib + 1) * sub, row_body, 0)


def _dn_solve(lt):
    ni = lt.shape[-1]
    spec = pl.BlockSpec((CHUNK, CHUNK, LANES), lambda i: (0, 0, i))
    return pl.pallas_call(
        _dn_solve_kernel,
        grid=(pl.cdiv(ni, LANES),),
        in_specs=[spec],
        out_specs=spec,
        out_shape=jax.ShapeDtypeStruct(lt.shape, F32),
        compiler_params=pltpu.CompilerParams(dimension_semantics=("arbitrary",)),
        name="dn_solve",
    )(lt)


def _dn_scan_kernel(qt_ref, rhs_ref, kd_ref, t_ref, a_ref, egl_ref, z_ref, nw_ref, y_ref, s_ref,
                    *, chunks, nb):
    @pl.when(pl.program_id(1) == 0)
    def _():
        s_ref[...] = jnp.zeros_like(s_ref)

    nw = nw_ref[...]
    units = [(b, h) for b in range(nb) for h in range(DN_HEADS)]
    uw = [[_dot(t_ref[b, c, h],
                rhs_ref[b, c * CHUNK:(c + 1) * CHUNK, 2 * h * DN_HEAD_DIM:(2 * h + 2) * DN_HEAD_DIM])
           for (b, h) in units] for c in range(chunks)]
    s = [s_ref[b * DN_HEADS + h] for (b, h) in units]
    for c in range(chunks):
        rc = slice(c * CHUNK, (c + 1) * CHUNK)
        ws = []
        for i, (b, h) in enumerate(units):
            hs = slice(h * DN_HEAD_DIM, (h + 1) * DN_HEAD_DIM)
            wq = jnp.concatenate([uw[c][i][:, DN_HEAD_DIM:].astype(BF16), qt_ref[b, rc, hs]], axis=0)
            ws.append(_dot(wq, s[i].astype(BF16)))
        for i, (b, h) in enumerate(units):
            hs = slice(h * DN_HEAD_DIM, (h + 1) * DN_HEAD_DIM)
            vn = (uw[c][i][:, :DN_HEAD_DIM] - ws[i][:CHUNK]).astype(BF16)
            o = ws[i][CHUNK:] + _dot(a_ref[b, c, h], vn)
            s[i] = s[i] * egl_ref[b, c, h:h + 1, :] + _dot_tn(kd_ref[b, rc, hs], vn)
            ms = jnp.mean(o * o, axis=-1, keepdims=True)
            on = o * lax.rsqrt(ms + NORM_EPS) * nw
            y_ref[b, rc, hs] = (on * _silu(z_ref[b, rc, hs].astype(F32))).astype(BF16)
    for i, (b, h) in enumerate(units):
        s_ref[b * DN_HEADS + h] = s[i]


def _dn_scan(qt, rhs, kd, tmat, amat, egl, main, norm_w, batch, nc, chunks):
    tp = nc * CHUNK
    nt = nc // chunks
    rt = chunks * CHUNK
    nb = 2 if batch % 2 == 0 else 1

    def seq_tile(width, col):
        return pl.BlockSpec((nb, rt, width), lambda p, t: (p, t, col))

    def inst(*tail):
        shape = (nb, chunks, DN_HEADS) + tail
        return pl.BlockSpec(shape, lambda p, t: (p, t) + (0,) * (len(shape) - 2))

    def by_seq(a):
        return a.reshape((batch, a.shape[0] // batch) + a.shape[1:])

    y = pl.pallas_call(
        functools.partial(_dn_scan_kernel, chunks=chunks, nb=nb),
        grid=(batch // nb, nt),
        in_specs=[
            seq_tile(D_MODEL, 0), seq_tile(2 * D_MODEL, 0), seq_tile(D_MODEL, 0), inst(CHUNK, CHUNK),
            inst(CHUNK, CHUNK), inst(LANES), seq_tile(D_MODEL, COL_DN_Z),
            pl.BlockSpec((1, DN_HEAD_DIM), lambda p, t: (0, 0), pipeline_mode=pl.Buffered(1)),
        ],
        out_specs=seq_tile(D_MODEL, 0),
        out_shape=jax.ShapeDtypeStruct((batch, tp, D_MODEL), BF16),
        scratch_shapes=[pltpu.VMEM((nb * DN_HEADS, DN_HEAD_DIM, DN_HEAD_DIM), F32)],
        compiler_params=pltpu.CompilerParams(
            dimension_semantics=("arbitrary", "arbitrary"), vmem_limit_bytes=VMEM_LIMIT),
        name="dn_scan",
    )(by_seq(qt), by_seq(rhs), by_seq(kd), by_seq(tmat), by_seq(amat), by_seq(egl), by_seq(main), norm_w)
    return y.reshape(batch * tp, D_MODEL)


def _ssd_kernel(xbc_ref, halo_ref, z_ref, small_ref, cw_ref, cb_ref, vec_ref, psel_ref,
                de_ref, nw_ref, smat_ref, y_ref, xe_ref, h_ref, *, chunks, nb):
    t = pl.program_id(1)
    rows = chunks * CHUNK
    nch = nb * chunks

    @pl.when(t == 0)
    def _():
        h_ref[...] = jnp.zeros_like(h_ref)

    for b in range(nb):
        _stage_rows(xbc_ref.at[b], halo_ref.at[b], xe_ref.at[b], t == 0)

    sm = jnp.concatenate([small_ref[b] for b in range(nb)], axis=0)
    lane = _iota((rows, LANES), 1)
    real = ((t * rows + _iota((rows, LANES), 0)) >= PAD_FRONT) & (lane < SSD_HEADS)
    dt = jnp.where(jnp.concatenate([real] * nb, axis=0), _softplus(sm + vec_ref[0:1, :]), 0.0)
    da = dt * (-jnp.exp(vec_ref[1:2, :]))
    tril = _chunk_tril_bf16(rows)
    acum = jnp.concatenate([_sel_dot(tril, da[b * rows:(b + 1) * rows]) for b in range(nb)], axis=0)
    ae = _expand_lanes(acum, LANE_DT, SSD_HEADS, SSD_HEAD_DIM)
    dte = _expand_lanes(dt, LANE_DT, SSD_HEADS, SSD_HEAD_DIM)
    alast = _last_row_of_chunks(ae, nch)

    conv = [[_silu(acc + cb_ref[:, cs])
             for cs, acc in _conv_blocks(xe_ref.at[b], smat_ref, cw_ref, SSD_XBC, rows)] for b in range(nb)]
    nx = D_MODEL // CONV_COLS
    xs = jnp.concatenate([jnp.concatenate(cb[:nx], axis=1) for cb in conv], axis=0)
    bc = jnp.concatenate([jnp.concatenate(cb[nx:], axis=1) for cb in conv], axis=0).astype(BF16)
    xc = xs * dte
    xcb = xc.astype(BF16)
    xdec = (xc * jnp.exp(alast - ae)).astype(BF16)
    eae = jnp.exp(ae)
    ealast = [jnp.exp(ae[(c + 1) * CHUNK - 1:(c + 1) * CHUNK, :]) for c in range(nch)]

    half = SSD_HEADS // SSD_GROUPS * SSD_HEAD_DIM
    pairs = SSD_HEADS // SSD_GROUPS // 2
    cr = [slice(c * CHUNK, (c + 1) * CHUNK) for c in range(nch)]
    gsl = [slice(g * half, (g + 1) * half) for g in range(SSD_GROUPS)]
    bgs = [[bc[cr[c], g * SSD_STATE:(g + 1) * SSD_STATE] for g in range(SSD_GROUPS)] for c in range(nch)]
    cgs = [[bc[cr[c], (SSD_GROUPS + g) * SSD_STATE:(SSD_GROUPS + g + 1) * SSD_STATE]
            for g in range(SSD_GROUPS)] for c in range(nch)]

    states = [[_dot_tn(bgs[c][g], xdec[cr[c], gsl[g]]) for g in range(SSD_GROUPS)] for c in range(nch)]
    yoff = [[None] * SSD_GROUPS for _ in range(nch)]
    for b in range(nb):
        for g in range(SSD_GROUPS):
            hg = h_ref[b * SSD_GROUPS + g]
            for c in range(b * chunks, (b + 1) * chunks):
                yoff[c][g] = _dot(cgs[c][g], hg.astype(BF16)) * eae[cr[c], gsl[g]]
                hg = hg * ealast[c][:, gsl[g]] + states[c][g]
            h_ref[b * SSD_GROUPS + g] = hg

    psel = psel_ref[...]
    prow = _iota((2 * CHUNK, LANES), 0)
    plane = _iota((2 * CHUNK, LANES), 1)
    keep = (prow < CHUNK) == ((plane & 1) == 0)
    bd_mask = (prow < CHUNK) == (plane < CHUNK)
    causal2 = _iota((CHUNK, LANES), 0) >= (_iota((CHUNK, LANES), 1) & (CHUNK - 1))
    cbcb = [[_dot_nt(cgs[c][g], jnp.concatenate([bgs[c][g], bgs[c][g]], axis=0))
             for g in range(SSD_GROUPS)] for c in range(nch)]
    rowpair = []
    for c in range(nch):
        ac = acum[cr[c]]
        a2 = jnp.where(keep, jnp.concatenate([ac, ac], axis=0), 0.0)
        rowpair.append(_sel_dot_nt(psel, a2))
    nw = nw_ref[...]
    de = de_ref[...]
    for c in range(nch):
        b, lc = c // chunks, slice((c % chunks) * CHUNK, (c % chunks + 1) * CHUNK)
        ys = []
        for g in range(SSD_GROUPS):
            for p in range(pairs):
                j = g * pairs + p
                js = slice(j * LANES, (j + 1) * LANES)
                diff = ae[cr[c], js] - rowpair[c][j:j + 1, :]
                m = jnp.where(causal2, cbcb[c][g] * jnp.exp(jnp.minimum(diff, 0.0)), 0.0).astype(BF16)
                xp = xcb[cr[c], js]
                bd = jnp.where(bd_mask, jnp.concatenate([xp, xp], axis=0), jnp.zeros((), BF16))
                ys.append(_dot(m, bd) + yoff[c][g][:, p * LANES:(p + 1) * LANES])
        y = jnp.concatenate(ys, axis=1) + de * xs[cr[c]]
        y = y * _silu(z_ref[b, lc, :].astype(F32))
        for g in range(SSD_GROUPS):
            yg = y[:, gsl[g]]
            ms = jnp.mean(yg * yg, axis=-1, keepdims=True)
            y_ref[b, lc, gsl[g]] = (yg * lax.rsqrt(ms + NORM_EPS) * nw[:, gsl[g]]).astype(BF16)


def _ssd(main, small, conv_w, conv_b, vec, psel, d_exp, norm_w, smat, batch, nc, chunks):
    tp = nc * CHUNK
    nt = nc // chunks
    rt = chunks * CHUNK
    nb = 2 if batch % 2 == 0 else 1
    _, _, const, _ = _row_specs(rt, nt)

    def seq_tile(width, col):
        return pl.BlockSpec((nb, rt, width), lambda p, t: (p, t, col))

    def seq_halo(width, col):
        return pl.BlockSpec((nb, HALO, width), lambda p, t: (p, jnp.maximum(t * (rt // HALO) - 1, 0), col))

    main3 = main.reshape(batch, tp, main.shape[1])
    y = pl.pallas_call(
        functools.partial(_ssd_kernel, chunks=chunks, nb=nb),
        grid=(batch // nb, nt),
        in_specs=[
            seq_tile(SSD_XBC, COL_XBC), seq_halo(SSD_XBC, COL_XBC), seq_tile(D_MODEL, COL_SSD_Z),
            seq_tile(LANES, 0),
            const((CONV_TAPS, SSD_XBC)), const((1, SSD_XBC)), const((8, LANES)),
            const((8, LANES)), const((1, D_MODEL)), const((1, D_MODEL)),
            const(smat.shape),
        ],
        out_specs=seq_tile(D_MODEL, 0),
        out_shape=jax.ShapeDtypeStruct((batch, tp, D_MODEL), BF16),
        scratch_shapes=[
            pltpu.VMEM((nb, HALO + rt, SSD_XBC), BF16),
            pltpu.VMEM((nb * SSD_GROUPS, SSD_STATE, D_MODEL // SSD_GROUPS), F32),
        ],
        compiler_params=pltpu.CompilerParams(
            dimension_semantics=("arbitrary", "arbitrary"), vmem_limit_bytes=VMEM_LIMIT),
        name="ssd",
    )(main3, main3, main3, small.reshape(batch, tp, LANES), conv_w, conv_b, vec, psel, d_exp, norm_w, smat)
    return y.reshape(batch * tp, D_MODEL)


def _out_kernel(yssd_ref, ydn_ref, b_ref, c_ref, chalo_ref, h_ref, hhalo_ref, g_ref, gates_ref,
                x_ref, front_ref, scw_ref, wb_ref, wo_ref, np_ref, o_ref, *, row0, x_unpadded):
    t = pl.program_id(1)
    rows = x_ref.shape[0]
    sub = 8
    u = c_ref[...].astype(F32) * h_ref[...].astype(F32)
    uh = chalo_ref[...].astype(F32) * hhalo_ref[...].astype(F32)
    if row0 == 0:
        uh = jnp.where(t == 0, 0.0, uh)
    w0, w1, w2 = scw_ref[0:1, :], scw_ref[1:2, :], scw_ref[2:3, :]
    conv = w2 * u + w1 * pltpu.roll(u, 1, 0) + w0 * pltpu.roll(u, 2, 0)
    head = jnp.concatenate([uh, u[0:sub]], axis=0)
    fix = (w2 * head[HALO:HALO + sub] + w1 * head[HALO - 1:HALO - 1 + sub]
           + w0 * head[HALO - 2:HALO - 2 + sub])
    conv = jnp.concatenate([fix, conv[sub:]], axis=0)
    ysc = (b_ref[...].astype(F32) * conv * _silu(g_ref[...].astype(F32))).astype(BF16)

    merged = None
    for n, y in enumerate((yssd_ref[...], ysc, ydn_ref[...])):
        gate = _sigmoid(gates_ref[:, n * D_MODEL:(n + 1) * D_MODEL].astype(F32))
        term = gate * _dot(y, wb_ref[n])
        merged = term if merged is None else merged + term
    out = _dot(merged.astype(BF16), wo_ref[...])
    ms = jnp.mean(out * out, axis=-1, keepdims=True)
    resid = x_ref[...]
    if x_unpadded and row0 == 0:
        resid = _with_front_rows(resid, front_ref, t == 0)
    res = resid + out * lax.rsqrt(ms + NORM_EPS) * np_ref[...]
    if row0 < PAD_FRONT:
        res = jnp.where((row0 + t * rows + _iota(res.shape, 0)) >= PAD_FRONT, res, 0.0)
    o_ref[...] = res


def _out(yssd, ydn, main, x, front, sc_w, w_branch, w_out, norm_post, batch, tp, rt, row0, x_unpadded):
    nt = (tp - row0) // rt
    rows = batch * nt * rt
    _, _, const, _ = _row_specs(rt, nt)
    if row0 == 0:
        tile, halo, _, _ = _row_specs(rt, nt)
    else:
        def tile(width, col):
            return pl.BlockSpec(
                (pl.Element(rt), pl.Element(width)),
                lambda b, t: ((b * (tp // HALO) + row0 // HALO + t * (rt // HALO)) * HALO, col * width))

        def halo(width, col):
            return pl.BlockSpec(
                (pl.Element(HALO), pl.Element(width)),
                lambda b, t: ((b * (tp // HALO) + row0 // HALO + t * (rt // HALO) - 1) * HALO, col * width))

    out_spec = pl.BlockSpec((rt, D_MODEL), lambda b, t: (b * nt + t, 0))
    if not x_unpadded:
        x_spec = tile(D_MODEL, 0)
    elif row0 == 0:
        x_spec = _unpadded_row_spec(rt, nt, tp - CHUNK)(lambda b, t: (b, t))
    else:
        x_spec = out_spec
    return pl.pallas_call(
        functools.partial(_out_kernel, row0=row0, x_unpadded=x_unpadded),
        grid=(batch, nt),
        in_specs=[
            tile(D_MODEL, 0), tile(D_MODEL, 0),
            tile(D_MODEL, COL_SC_B), tile(D_MODEL, COL_SC_C), halo(D_MODEL, COL_SC_C),
            tile(D_MODEL, COL_SC_H), halo(D_MODEL, COL_SC_H), tile(D_MODEL, COL_SC_G),
            tile(3 * D_MODEL, COL_GATES), x_spec, const((CHUNK, D_MODEL)),
            const((3, D_MODEL)), const((3, D_MODEL, D_MODEL)), const((D_MODEL, D_MODEL)),
            const((1, D_MODEL)),
        ],
        out_specs=out_spec,
        out_shape=jax.ShapeDtypeStruct((rows, D_MODEL), F32),
        compiler_params=pltpu.CompilerParams(
            dimension_semantics=("arbitrary", "arbitrary"), vmem_limit_bytes=VMEM_LIMIT),
        name="out_proj",
    )(yssd, ydn, main, main, main, main, main, main, main, x, front, sc_w, w_branch, w_out, norm_post)


def _lane_vec(rows):
    padded = [jnp.pad(v.astype(F32), (off, LANES - off - v.shape[0])) for off, v in rows]
    padded += [jnp.zeros((LANES,), F32)] * (8 - len(padded))
    return jnp.stack(padded)


def _shift_matrix(rows):
    r = jnp.arange((CONV_TAPS - 1) * rows)[:, None]
    c = jnp.arange(HALO + rows)[None, :]
    return (c == HALO - (CONV_TAPS - 1) + r % rows + r // rows).astype(BF16)


def _split_w_in(w):
    gates = w[:, IN_OFFS[11]:IN_OFFS[12]]
    qkv_dnz = w[:, IN_OFFS[7]:IN_OFFS[9]]
    sc = w[:, IN_OFFS[3]:IN_OFFS[7]]
    ssdz_xbc = w[:, IN_OFFS[0]:IN_OFFS[2]]
    return jnp.concatenate([p.astype(BF16) for p in (gates, qkv_dnz, sc, ssdz_xbc)], axis=1)


def kernel(x, meta_tokens, norm_pre, norm_post, w_in, ssd_conv_w, ssd_conv_b, ssd_dt_bias, ssd_a_log,
           ssd_d, ssd_norm, sc_conv_w, dn_conv_w, dn_dt_bias, dn_a_log, dn_norm, w_branch, w_out):
    batch, seq, d = x.shape
    assert d == D_MODEL and seq % CHUNK == 0
    tp = CHUNK + seq
    nc = tp // CHUNK
    chunks = 3 if nc % 3 == 0 else 1
    rows = batch * tp
    tm = tp // 2 if tp % 32 == 0 else tp
    rt_out = tp // 4 if tp % (4 * HALO) == 0 else tp
    rt_last = seq // 4 if seq % (4 * CHUNK) == 0 else CHUNK
    depth = w_in.shape[0]

    front = jnp.concatenate([jnp.zeros((PAD_FRONT, d), x.dtype), meta_tokens.astype(x.dtype)], axis=0)
    h = x.reshape(batch * seq, d)

    smat = _shift_matrix(chunks * CHUNK)
    pj = jnp.arange(8)[:, None]
    pc = jnp.arange(LANES)[None, :]
    psel = ((pc // 2 == pj) & (pc < SSD_HEADS)).astype(BF16)

    for i in range(depth):
        unpadded = i == 0
        dt0 = OFF_DT // LANES * LANES
        ba0 = OFF_BETA // LANES * LANES
        main, small = _in_proj(h, front, norm_pre[i][None, :], _split_w_in(w_in[i]),
                               w_in[i, :, dt0:dt0 + LANES], w_in[i, :, ba0:ba0 + LANES], tm, tp, unpadded)

        dn_vec = _lane_vec([(LANE_A, dn_dt_bias[i]), (LANE_A, dn_a_log[i])])
        qt, rhs, kd, lmat, amat, egl = _dn_prep(
            main, small, dn_conv_w[i], dn_vec, smat, batch, nc, chunks)
        ni = batch * nc * DN_HEADS
        lt = jnp.transpose(lmat.reshape(ni, CHUNK, CHUNK), (1, 2, 0))
        if ni < LANES:
            lt = jnp.pad(lt, ((0, 0), (0, 0), (0, LANES - ni)))

        ssd_vec = _lane_vec([(LANE_DT, ssd_dt_bias[i]), (LANE_DT, ssd_a_log[i])])
        d_exp = jnp.repeat(ssd_d[i].astype(F32), SSD_HEAD_DIM)[None, :]
        yssd = _ssd(main, small, ssd_conv_w[i], ssd_conv_b[i][None, :], ssd_vec, psel, d_exp,
                    ssd_norm[i][None, :], smat, batch, nc, chunks)

        tmat = jnp.transpose(_dn_solve(lt)[:, :, :ni], (2, 0, 1)).astype(BF16).reshape(
            batch * nc, DN_HEADS, CHUNK, CHUNK)
        ydn = _dn_scan(qt, rhs, kd, tmat, amat, egl, main, dn_norm[i][None, :], batch, nc, chunks)

        last = i == depth - 1
        h = _out(yssd, ydn, main, h, front, sc_conv_w[i], w_branch[i].astype(BF16), w_out[i].astype(BF16),
                 norm_post[i][None, :], batch, tp, rt_last if last else rt_out, CHUNK if last else 0,
                 unpadded)

    return h.reshape(batch, seq, d)
```

```python
import functools

import jax
import jax.numpy as jnp
from jax import lax
from jax.experimental import pallas as pl
from jax.experimental.pallas import tpu as pltpu

F32 = jnp.float32
BF16 = jnp.bfloat16

D_MODEL = 1024
N_META = 16
CHUNK = 64
CHUNK_SHIFT = 6
PAD_FRONT = CHUNK - N_META
NORM_EPS = 1e-6
SSD_HEADS = 16
SSD_HEAD_DIM = 64
SSD_GROUPS = 2
SSD_STATE = 128
SSD_XBC = 1536
DN_HEADS = 8
DN_HEAD_DIM = 128
CONV_TAPS = 4
LANES = 128
HALO = 16
CONV_COLS = 256
PROJ_TN = 2304
VMEM_LIMIT = 52 * 1024 * 1024

MAIN_WIDTH = 13824
COL_GATES = 0
COL_QKV = 1
COL_DN_Z = 6
COL_SC_B = 7
COL_SC_C = 8
COL_SC_H = 9
COL_SC_G = 10
COL_SSD_Z = 11
COL_XBC = 8
IN_SIZES = (1024, 1536, 16, 1024, 1024, 1024, 1024, 3072, 1024, 8, 8, 3072)
IN_OFFS = tuple(sum(IN_SIZES[:i]) for i in range(len(IN_SIZES) + 1))
OFF_DT, OFF_BETA, OFF_A = IN_OFFS[2], IN_OFFS[9], IN_OFFS[10]
LANE_DT = OFF_DT % LANES
LANE_BETA = OFF_BETA % LANES
LANE_A = OFF_A % LANES
assert OFF_BETA // LANES == OFF_A // LANES and LANE_DT + SSD_HEADS <= LANE_BETA < LANE_A


def _dot(a, b):
    return jnp.dot(a, b, preferred_element_type=F32)


def _dot_nt(a, b):
    return lax.dot_general(a, b, (((1,), (1,)), ((), ())), preferred_element_type=F32)


def _dot_tn(a, b):
    return lax.dot_general(a, b, (((0,), (0,)), ((), ())), preferred_element_type=F32)


def _split3(x):
    hi = x.astype(BF16)
    r1 = x - hi.astype(F32)
    mid = r1.astype(BF16)
    lo = (r1 - mid.astype(F32)).astype(BF16)
    return hi, mid, lo


def _expand_lanes(x, lane0, heads, width):
    rows = x.shape[0]
    cols = [jnp.broadcast_to(x[:, lane0 + h:lane0 + h + 1], (rows, LANES)) for h in range(heads)]
    if width == LANES:
        return jnp.concatenate(cols, axis=1)
    low = _iota((rows, LANES), 1) < width
    return jnp.concatenate([jnp.where(low, cols[h], cols[h + 1]) for h in range(0, heads, 2)], axis=1)


def _sel_dot(sel, x):
    hi, mid, lo = _split3(x)
    return _dot(sel, hi) + _dot(sel, mid) + _dot(sel, lo)


def _sigmoid(x):
    return 1.0 / (1.0 + jnp.exp(-x))


def _silu(x):
    return x * _sigmoid(x)


def _softplus(x):
    return jnp.maximum(x, 0.0) + jnp.log(1.0 + jnp.exp(-jnp.abs(x)))


def _iota(shape, dim):
    return lax.broadcasted_iota(jnp.int32, shape, dim)


def _chunk_tril_bf16(rows):
    r = _iota((rows, rows), 0)
    c = _iota((rows, rows), 1)
    same = lax.shift_right_logical(r, CHUNK_SHIFT) == lax.shift_right_logical(c, CHUNK_SHIFT)
    return ((r >= c) & same).astype(BF16)


def _last_row_of_chunks(x, chunks):
    return jnp.concatenate(
        [jnp.broadcast_to(x[(c + 1) * CHUNK - 1:(c + 1) * CHUNK, :], (CHUNK, x.shape[1]))
         for c in range(chunks)], axis=0)


def _stage_rows(x_ref, halo_ref, xe_ref, first):
    rows = x_ref.shape[0]
    halo = halo_ref[...]
    xe_ref[0:HALO, :] = jnp.where(first, jnp.zeros_like(halo), halo)
    xe_ref[HALO:HALO + rows, :] = x_ref[...]


def _conv_blocks(xe_ref, smat_ref, w_ref, width, rows):
    slices = [slice(j * CONV_COLS, (j + 1) * CONV_COLS) for j in range(width // CONV_COLS)]
    sh_next = _dot(smat_ref[...], xe_ref[:, slices[0]])
    for j, cs in enumerate(slices):
        sh = sh_next
        if j + 1 < len(slices):
            sh_next = _dot(smat_ref[...], xe_ref[:, slices[j + 1]])
        acc = w_ref[CONV_TAPS - 1:CONV_TAPS, cs] * xe_ref[HALO:HALO + rows, cs].astype(F32)
        for k in range(CONV_TAPS - 1):
            acc = acc + w_ref[k:k + 1, cs] * sh[k * rows:(k + 1) * rows]
        yield cs, acc


def _with_front_rows(x, front_ref, is_first):
    shifted = jnp.concatenate([front_ref[...], x[:x.shape[0] - CHUNK]], axis=0)
    return jnp.where(is_first, shifted, x)


def _unpadded_row_spec(rt, tiles_per_batch, seq):
    def index(b, t):
        row = b * (seq // HALO) + jnp.maximum(t * (rt // HALO) - CHUNK // HALO, 0)
        return row * HALO, 0

    return lambda grid_to_bt: pl.BlockSpec(
        (pl.Element(rt), pl.Element(D_MODEL)), lambda *g: index(*grid_to_bt(*g)))


def _in_proj_kernel(x_ref, front_ref, gain_ref, w_ref, wdt_ref, wba_ref, main_ref, small_ref, xn_ref,
                    *, unpadded, tiles_per_batch):
    @pl.when(pl.program_id(1) == 0)
    def _():
        x = x_ref[...]
        if unpadded:
            x = _with_front_rows(x, front_ref, pl.program_id(0) % tiles_per_batch == 0)
        ms = jnp.mean(x * x, axis=-1, keepdims=True)
        xn = (x * lax.rsqrt(ms + NORM_EPS) * gain_ref[...]).astype(BF16)
        xn_ref[...] = xn
        lane = _iota(wdt_ref.shape, 1)
        ws = jnp.where(lane < LANE_DT + SSD_HEADS, wdt_ref[...],
                       jnp.where((lane >= LANE_BETA) & (lane < LANE_A + DN_HEADS), wba_ref[...], 0.0))
        small_ref[...] = _dot(xn, ws.astype(BF16))

    main_ref[...] = _dot(xn_ref[...], w_ref[...]).astype(BF16)


def _in_proj(h, front, gain, w_main, w_dt_blk, w_ba_blk, tm, tp, unpadded):
    tiles_per_batch = tp // tm
    rows = h.shape[0] // (tp - CHUNK) * tp if unpadded else h.shape[0]
    grid = (rows // tm, MAIN_WIDTH // PROJ_TN)
    if unpadded:
        x_spec = _unpadded_row_spec(tm, tiles_per_batch, tp - CHUNK)(
            lambda i, j: (i // tiles_per_batch, i % tiles_per_batch))
    else:
        x_spec = pl.BlockSpec((tm, D_MODEL), lambda i, j: (i, 0))
    return pl.pallas_call(
        functools.partial(_in_proj_kernel, unpadded=unpadded, tiles_per_batch=tiles_per_batch),
        grid=grid,
        in_specs=[
            x_spec,
            pl.BlockSpec((CHUNK, D_MODEL), lambda i, j: (0, 0)),
            pl.BlockSpec((1, D_MODEL), lambda i, j: (0, 0)),
            pl.BlockSpec((D_MODEL, PROJ_TN), lambda i, j: (0, j)),
            pl.BlockSpec((D_MODEL, LANES), lambda i, j: (0, 0)),
            pl.BlockSpec((D_MODEL, LANES), lambda i, j: (0, 0)),
        ],
        out_specs=[
            pl.BlockSpec((tm, PROJ_TN), lambda i, j: (i, j)),
            pl.BlockSpec((tm, LANES), lambda i, j: (i, 0)),
        ],
        out_shape=[
            jax.ShapeDtypeStruct((rows, MAIN_WIDTH), BF16),
            jax.ShapeDtypeStruct((rows, LANES), F32),
        ],
        scratch_shapes=[pltpu.VMEM((tm, D_MODEL), BF16)],
        compiler_params=pltpu.CompilerParams(
            dimension_semantics=("arbitrary", "arbitrary"), vmem_limit_bytes=VMEM_LIMIT),
        name="in_proj",
    )(h, front, gain, w_main, w_dt_blk, w_ba_blk)


def _row_specs(rt, nt):
    def tile(width, col):
        return pl.BlockSpec((rt, width), lambda b, t: (b * nt + t, col))

    def halo(width, col):
        return pl.BlockSpec(
            (HALO, width), lambda b, t: (jnp.maximum((b * nt + t) * (rt // HALO) - 1, 0), col))

    def const(shape):
        return pl.BlockSpec(shape, lambda b, t: (0,) * len(shape), pipeline_mode=pl.Buffered(1))

    def inst(*tail):
        shape = (rt // CHUNK, DN_HEADS) + tail
        return pl.BlockSpec(shape, lambda b, t: (b * nt + t,) + (0,) * (len(shape) - 1))

    return tile, halo, const, inst


def _dn_prep_kernel(qkv_ref, halo_ref, small_ref, cw_ref, vec_ref, smat_ref,
                    qt_ref, rhs_ref, kd_ref, l_ref, a_ref, egl_ref, xe_ref, *, chunks):
    t = pl.program_id(1)
    rows = chunks * CHUNK
    _stage_rows(qkv_ref, halo_ref, xe_ref, t == 0)

    sm = small_ref[...]
    lane = _iota(sm.shape, 1)
    real = (t * rows + _iota(sm.shape, 0)) >= PAD_FRONT
    is_beta = (lane >= LANE_BETA) & (lane < LANE_BETA + DN_HEADS)
    is_a = (lane >= LANE_A) & (lane < LANE_A + DN_HEADS)
    g = -jnp.exp(vec_ref[1:2, :]) * _softplus(sm + vec_ref[0:1, :])
    g = jnp.where(real & is_a, g, 0.0)
    beta = jnp.where(real & is_beta, _sigmoid(sm), 0.0)
    gcum = _sel_dot(_chunk_tril_bf16(rows), g)
    ge = _expand_lanes(gcum, LANE_A, DN_HEADS, DN_HEAD_DIM)
    be = _expand_lanes(beta, LANE_BETA, DN_HEADS, DN_HEAD_DIM)
    gl = _last_row_of_chunks(ge, chunks)
    eg = jnp.exp(ge)
    ekd = jnp.exp(gl - ge)

    li = _iota((CHUNK, CHUNK), 0)
    si = _iota((CHUNK, CHUNK), 1)
    causal = li >= si
    strict = li > si
    cr = [slice(c * CHUNK, (c + 1) * CHUNK) for c in range(chunks)]
    egl = [jnp.exp(ge[cr[c].stop - 1:cr[c].stop, :]) for c in range(chunks)]
    gct = [jnp.transpose(gcum[cr[c]]) for c in range(chunks)]

    def l2n(vh, scale):
        ss = jnp.sum(vh * vh, axis=-1, keepdims=True)
        return vh * (lax.rsqrt(ss + NORM_EPS) * scale)

    heads_per_block = CONV_COLS // DN_HEAD_DIM
    qkv = []
    for j, (cs, acc) in enumerate(_conv_blocks(xe_ref, smat_ref, cw_ref, 3 * D_MODEL, rows)):
        val = _silu(acc)
        for i in range(heads_per_block):
            vh = val[:, i * DN_HEAD_DIM:(i + 1) * DN_HEAD_DIM]
            if j < D_MODEL // CONV_COLS:
                vh = l2n(vh, DN_HEAD_DIM ** -0.5)
            elif j < 2 * D_MODEL // CONV_COLS:
                vh = l2n(vh, 1.0)
            qkv.append(vh)
    qs, ks, vs = qkv[:DN_HEADS], qkv[DN_HEADS:2 * DN_HEADS], qkv[2 * DN_HEADS:]

    for h in range(DN_HEADS):
        hs = slice(h * DN_HEAD_DIM, (h + 1) * DN_HEAD_DIM)
        bt = be[:, hs]
        qt_ref[:, hs] = (qs[h] * eg[:, hs]).astype(BF16)
        rhs_ref[:, 2 * h * DN_HEAD_DIM:(2 * h + 1) * DN_HEAD_DIM] = (vs[h] * bt).astype(BF16)
        rhs_ref[:, (2 * h + 1) * DN_HEAD_DIM:(2 * h + 2) * DN_HEAD_DIM] = (ks[h] * (bt * eg[:, hs])).astype(BF16)
        kd_ref[:, hs] = (ks[h] * ekd[:, hs]).astype(BF16)

    for c in range(chunks):
        for h in range(DN_HEADS):
            c0 = h * DN_HEAD_DIM
            kb = ks[h][cr[c]].astype(BF16)
            kq = _dot_nt(jnp.concatenate([kb, qs[h][cr[c]].astype(BF16)], axis=0), kb)
            gs = gct[c][LANE_A + h:LANE_A + h + 1, :]
            decay = jnp.where(causal, jnp.exp(jnp.minimum(ge[cr[c], c0:c0 + CHUNK] - gs, 0.0)), 0.0)
            l_ref[c, h] = jnp.where(strict, be[cr[c], c0:c0 + CHUNK] * kq[:CHUNK] * decay, 0.0)
            a_ref[c, h] = (kq[CHUNK:] * decay).astype(BF16)
            egl_ref[c, h:h + 1, :] = egl[c][:, c0:c0 + LANES]


def _dn_prep(main, small, conv_w, vec, smat, batch, nc, chunks):
    rows = main.shape[0]
    nt = nc // chunks
    rt = chunks * CHUNK
    tile, halo, const, inst = _row_specs(rt, nt)
    return pl.pallas_call(
        functools.partial(_dn_prep_kernel, chunks=chunks),
        grid=(batch, nt),
        in_specs=[
            tile(3 * D_MODEL, COL_QKV), halo(3 * D_MODEL, COL_QKV), tile(LANES, 0),
            const((CONV_TAPS, 3 * D_MODEL)), const((8, LANES)), const(smat.shape),
        ],
        out_specs=[
            tile(D_MODEL, 0), tile(2 * D_MODEL, 0), tile(D_MODEL, 0),
            inst(CHUNK, CHUNK), inst(CHUNK, CHUNK), inst(LANES),
        ],
        out_shape=[
            jax.ShapeDtypeStruct((rows, D_MODEL), BF16),
            jax.ShapeDtypeStruct((rows, 2 * D_MODEL), BF16),
            jax.ShapeDtypeStruct((rows, D_MODEL), BF16),
            jax.ShapeDtypeStruct((batch * nc, DN_HEADS, CHUNK, CHUNK), F32),
            jax.ShapeDtypeStruct((batch * nc, DN_HEADS, CHUNK, CHUNK), BF16),
            jax.ShapeDtypeStruct((batch * nc, DN_HEADS, LANES), F32),
        ],
        scratch_shapes=[pltpu.VMEM((HALO + rt, 3 * D_MODEL), BF16)],
        compiler_params=pltpu.CompilerParams(
            dimension_semantics=("arbitrary", "arbitrary"), vmem_limit_bytes=VMEM_LIMIT),
        name="dn_prep",
    )(main, main, small, conv_w, vec, smat)


def _dn_solve_kernel(lt_ref, tt_ref):
    sub = 8
    nblk = CHUNK // sub
    tt_ref[...] = jnp.zeros_like(tt_ref)
    col = _iota((sub, LANES), 0)

    for ib in range(nblk):
        def row_body(i, carry, ib=ib):
            acc = [(col + k * sub == i).astype(F32) for k in range(ib + 1)]
            for jb in range(ib + 1):
                blk = lt_ref[i, jb * sub:(jb + 1) * sub, :]
                for jj in range(sub):
                    j = jb * sub + jj
                    lij = blk[jj:jj + 1, :]
                    for k in range(jb + 1):
                        acc[k] = acc[k] - lij * tt_ref[j, k * sub:(k + 1) * sub, :]
            tt_ref[i, 0:(ib + 1) * sub, :] = jnp.concatenate(acc, axis=0)
            return carry

        lax.fori_loop(ib * sub, (ib + 1) * sub, row_body, 0)


def _dn_solve(lt):
    ni = lt.shape[-1]
    spec = pl.BlockSpec((CHUNK, CHUNK, LANES), lambda i: (0, 0, i))
    return pl.pallas_call(
        _dn_solve_kernel,
        grid=(pl.cdiv(ni, LANES),),
        in_specs=[spec],
        out_specs=spec,
        out_shape=jax.ShapeDtypeStruct(lt.shape, F32),
        compiler_params=pltpu.CompilerParams(dimension_semantics=("arbitrary",)),
        name="dn_solve",
    )(lt)


def _dn_scan_kernel(qt_ref, rhs_ref, kd_ref, t_ref, a_ref, egl_ref, z_ref, nw_ref, y_ref, s_ref,
                    *, chunks, nb):
    @pl.when(pl.program_id(1) == 0)
    def _():
        s_ref[...] = jnp.zeros_like(s_ref)

    nw = nw_ref[...]
    units = [(b, h) for b in range(nb) for h in range(DN_HEADS)]
    uw = [[_dot(t_ref[b, c, h],
                rhs_ref[b, c * CHUNK:(c + 1) * CHUNK, 2 * h * DN_HEAD_DIM:(2 * h + 2) * DN_HEAD_DIM])
           for (b, h) in units] for c in range(chunks)]
    s = [s_ref[b * DN_HEADS + h] for (b, h) in units]
    for c in range(chunks):
        rc = slice(c * CHUNK, (c + 1) * CHUNK)
        ws = []
        for i, (b, h) in enumerate(units):
            hs = slice(h * DN_HEAD_DIM, (h + 1) * DN_HEAD_DIM)
            wq = jnp.concatenate([uw[c][i][:, DN_HEAD_DIM:].astype(BF16), qt_ref[b, rc, hs]], axis=0)
            ws.append(_dot(wq, s[i].astype(BF16)))
        for i, (b, h) in enumerate(units):
            hs = slice(h * DN_HEAD_DIM, (h + 1) * DN_HEAD_DIM)
            vn = (uw[c][i][:, :DN_HEAD_DIM] - ws[i][:CHUNK]).astype(BF16)
            o = ws[i][CHUNK:] + _dot(a_ref[b, c, h], vn)
            s[i] = s[i] * egl_ref[b, c, h:h + 1, :] + _dot_tn(kd_ref[b, rc, hs], vn)
            ms = jnp.mean(o * o, axis=-1, keepdims=True)
            on = o * lax.rsqrt(ms + NORM_EPS) * nw
            y_ref[b, rc, hs] = (on * _silu(z_ref[b, rc, hs].astype(F32))).astype(BF16)
    for i, (b, h) in enumerate(units):
        s_ref[b * DN_HEADS + h] = s[i]


def _dn_scan(qt, rhs, kd, tmat, amat, egl, main, norm_w, batch, nc, chunks):
    tp = nc * CHUNK
    nt = nc // chunks
    rt = chunks * CHUNK
    nb = 2 if batch % 2 == 0 else 1

    def seq_tile(width, col):
        return pl.BlockSpec((nb, rt, width), lambda p, t: (p, t, col))

    def inst(*tail):
        shape = (nb, chunks, DN_HEADS) + tail
        return pl.BlockSpec(shape, lambda p, t: (p, t) + (0,) * (len(shape) - 2))

    def by_seq(a):
        return a.reshape((batch, a.shape[0] // batch) + a.shape[1:])

    y = pl.pallas_call(
        functools.partial(_dn_scan_kernel, chunks=chunks, nb=nb),
        grid=(batch // nb, nt),
        in_specs=[
            seq_tile(D_MODEL, 0), seq_tile(2 * D_MODEL, 0), seq_tile(D_MODEL, 0), inst(CHUNK, CHUNK),
            inst(CHUNK, CHUNK), inst(LANES), seq_tile(D_MODEL, COL_DN_Z),
            pl.BlockSpec((1, DN_HEAD_DIM), lambda p, t: (0, 0), pipeline_mode=pl.Buffered(1)),
        ],
        out_specs=seq_tile(D_MODEL, 0),
        out_shape=jax.ShapeDtypeStruct((batch, tp, D_MODEL), BF16),
        scratch_shapes=[pltpu.VMEM((nb * DN_HEADS, DN_HEAD_DIM, DN_HEAD_DIM), F32)],
        compiler_params=pltpu.CompilerParams(
            dimension_semantics=("arbitrary", "arbitrary"), vmem_limit_bytes=VMEM_LIMIT),
        name="dn_scan",
    )(by_seq(qt), by_seq(rhs), by_seq(kd), by_seq(tmat), by_seq(amat), by_seq(egl), by_seq(main), norm_w)
    return y.reshape(batch * tp, D_MODEL)


def _ssd_kernel(xbc_ref, halo_ref, z_ref, small_ref, cw_ref, cb_ref, vec_ref,
                de_ref, nw_ref, smat_ref, y_ref, xe_ref, h_ref, *, chunks, nb):
    t = pl.program_id(1)
    rows = chunks * CHUNK
    nch = nb * chunks

    @pl.when(t == 0)
    def _():
        h_ref[...] = jnp.zeros_like(h_ref)

    for b in range(nb):
        _stage_rows(xbc_ref.at[b], halo_ref.at[b], xe_ref.at[b], t == 0)

    sm = jnp.concatenate([small_ref[b] for b in range(nb)], axis=0)
    lane = _iota((rows, LANES), 1)
    real = ((t * rows + _iota((rows, LANES), 0)) >= PAD_FRONT) & (lane < SSD_HEADS)
    dt = jnp.where(jnp.concatenate([real] * nb, axis=0), _softplus(sm + vec_ref[0:1, :]), 0.0)
    da = dt * (-jnp.exp(vec_ref[1:2, :]))
    tril = _chunk_tril_bf16(rows)
    acum = jnp.concatenate([_sel_dot(tril, da[b * rows:(b + 1) * rows]) for b in range(nb)], axis=0)
    ae = _expand_lanes(acum, LANE_DT, SSD_HEADS, SSD_HEAD_DIM)
    dte = _expand_lanes(dt, LANE_DT, SSD_HEADS, SSD_HEAD_DIM)
    alast = _last_row_of_chunks(ae, nch)

    conv = [[_silu(acc + cb_ref[:, cs])
             for cs, acc in _conv_blocks(xe_ref.at[b], smat_ref, cw_ref, SSD_XBC, rows)] for b in range(nb)]
    nx = D_MODEL // CONV_COLS
    xs = jnp.concatenate([jnp.concatenate(cb[:nx], axis=1) for cb in conv], axis=0)
    bc = jnp.concatenate([jnp.concatenate(cb[nx:], axis=1) for cb in conv], axis=0).astype(BF16)
    xc = xs * dte
    xcb = xc.astype(BF16)
    xdec = (xc * jnp.exp(alast - ae)).astype(BF16)
    eae = jnp.exp(ae)
    ealast = [jnp.exp(ae[(c + 1) * CHUNK - 1:(c + 1) * CHUNK, :]) for c in range(nch)]

    half = SSD_HEADS // SSD_GROUPS * SSD_HEAD_DIM
    pairs = SSD_HEADS // SSD_GROUPS // 2
    cr = [slice(c * CHUNK, (c + 1) * CHUNK) for c in range(nch)]
    gsl = [slice(g * half, (g + 1) * half) for g in range(SSD_GROUPS)]
    bgs = [[bc[cr[c], g * SSD_STATE:(g + 1) * SSD_STATE] for g in range(SSD_GROUPS)] for c in range(nch)]
    cgs = [[bc[cr[c], (SSD_GROUPS + g) * SSD_STATE:(SSD_GROUPS + g + 1) * SSD_STATE]
            for g in range(SSD_GROUPS)] for c in range(nch)]

    states = [[_dot_tn(bgs[c][g], xdec[cr[c], gsl[g]]) for g in range(SSD_GROUPS)] for c in range(nch)]
    yoff = [[None] * SSD_GROUPS for _ in range(nch)]
    for b in range(nb):
        for g in range(SSD_GROUPS):
            hg = h_ref[b * SSD_GROUPS + g]
            for c in range(b * chunks, (b + 1) * chunks):
                yoff[c][g] = _dot(cgs[c][g], hg.astype(BF16)) * eae[cr[c], gsl[g]]
                hg = hg * ealast[c][:, gsl[g]] + states[c][g]
            h_ref[b * SSD_GROUPS + g] = hg

    prow = _iota((2 * CHUNK, LANES), 0)
    plane = _iota((2 * CHUNK, LANES), 1)
    keep = (prow < CHUNK) == ((plane & 1) == 0)
    bd_mask = (prow < CHUNK) == (plane < CHUNK)
    causal2 = _iota((CHUNK, LANES), 0) >= (_iota((CHUNK, LANES), 1) & (CHUNK - 1))
    cbcb = [[_dot_nt(cgs[c][g], jnp.concatenate([bgs[c][g], bgs[c][g]], axis=0))
             for g in range(SSD_GROUPS)] for c in range(nch)]
    rowpair = []
    for c in range(nch):
        ac = acum[cr[c]]
        a2t = jnp.transpose(jnp.where(keep, jnp.concatenate([ac, ac], axis=0), 0.0))
        rowpair.append([a2t[2 * j:2 * j + 1, :] + a2t[2 * j + 1:2 * j + 2, :] for j in range(SSD_HEADS // 2)])
    nw = nw_ref[...]
    de = de_ref[...]
    for c in range(nch):
        b, lc = c // chunks, slice((c % chunks) * CHUNK, (c % chunks + 1) * CHUNK)
        ys = []
        for g in range(SSD_GROUPS):
            for p in range(pairs):
                j = g * pairs + p
                js = slice(j * LANES, (j + 1) * LANES)
                diff = ae[cr[c], js] - rowpair[c][j]
                m = jnp.where(causal2, cbcb[c][g] * jnp.exp(jnp.minimum(diff, 0.0)), 0.0).astype(BF16)
                xp = xcb[cr[c], js]
                bd = jnp.where(bd_mask, jnp.concatenate([xp, xp], axis=0), jnp.zeros((), BF16))
                ys.append(_dot(m, bd) + yoff[c][g][:, p * LANES:(p + 1) * LANES])
        y = jnp.concatenate(ys, axis=1) + de * xs[cr[c]]
        y = y * _silu(z_ref[b, lc, :].astype(F32))
        for g in range(SSD_GROUPS):
            yg = y[:, gsl[g]]
            ms = jnp.mean(yg * yg, axis=-1, keepdims=True)
            y_ref[b, lc, gsl[g]] = (yg * lax.rsqrt(ms + NORM_EPS) * nw[:, gsl[g]]).astype(BF16)


def _ssd(main, small, conv_w, conv_b, vec, d_exp, norm_w, smat, batch, nc, chunks):
    tp = nc * CHUNK
    nt = nc // chunks
    rt = chunks * CHUNK
    nb = 2 if batch % 2 == 0 else 1
    _, _, const, _ = _row_specs(rt, nt)

    def seq_tile(width, col):
        return pl.BlockSpec((nb, rt, width), lambda p, t: (p, t, col))

    def seq_halo(width, col):
        return pl.BlockSpec((nb, HALO, width), lambda p, t: (p, jnp.maximum(t * (rt // HALO) - 1, 0), col))

    main3 = main.reshape(batch, tp, main.shape[1])
    y = pl.pallas_call(
        functools.partial(_ssd_kernel, chunks=chunks, nb=nb),
        grid=(batch // nb, nt),
        in_specs=[
            seq_tile(SSD_XBC, COL_XBC), seq_halo(SSD_XBC, COL_XBC), seq_tile(D_MODEL, COL_SSD_Z),
            seq_tile(LANES, 0),
            const((CONV_TAPS, SSD_XBC)), const((1, SSD_XBC)), const((8, LANES)),
            const((1, D_MODEL)), const((1, D_MODEL)),
            const(smat.shape),
        ],
        out_specs=seq_tile(D_MODEL, 0),
        out_shape=jax.ShapeDtypeStruct((batch, tp, D_MODEL), BF16),
        scratch_shapes=[
            pltpu.VMEM((nb, HALO + rt, SSD_XBC), BF16),
            pltpu.VMEM((nb * SSD_GROUPS, SSD_STATE, D_MODEL // SSD_GROUPS), F32),
        ],
        compiler_params=pltpu.CompilerParams(
            dimension_semantics=("arbitrary", "arbitrary"), vmem_limit_bytes=VMEM_LIMIT),
        name="ssd",
    )(main3, main3, main3, small.reshape(batch, tp, LANES), conv_w, conv_b, vec, d_exp, norm_w, smat)
    return y.reshape(batch * tp, D_MODEL)


def _out_kernel(yssd_ref, ydn_ref, b_ref, c_ref, chalo_ref, h_ref, hhalo_ref, g_ref, gates_ref,
                x_ref, front_ref, scw_ref, wb_ref, wo_ref, np_ref, o_ref, *, row0, x_unpadded):
    t = pl.program_id(1)
    rows = x_ref.shape[0]
    sub = 8
    u = c_ref[...].astype(F32) * h_ref[...].astype(F32)
    uh = chalo_ref[...].astype(F32) * hhalo_ref[...].astype(F32)
    if row0 == 0:
        uh = jnp.where(t == 0, 0.0, uh)
    w0, w1, w2 = scw_ref[0:1, :], scw_ref[1:2, :], scw_ref[2:3, :]
    conv = w2 * u + w1 * pltpu.roll(u, 1, 0) + w0 * pltpu.roll(u, 2, 0)
    head = jnp.concatenate([uh, u[0:sub]], axis=0)
    fix = (w2 * head[HALO:HALO + sub] + w1 * head[HALO - 1:HALO - 1 + sub]
           + w0 * head[HALO - 2:HALO - 2 + sub])
    conv = jnp.concatenate([fix, conv[sub:]], axis=0)
    ysc = (b_ref[...].astype(F32) * conv * _silu(g_ref[...].astype(F32))).astype(BF16)

    merged = None
    for n, y in enumerate((yssd_ref[...], ysc, ydn_ref[...])):
        gate = _sigmoid(gates_ref[:, n * D_MODEL:(n + 1) * D_MODEL].astype(F32))
        term = gate * _dot(y, wb_ref[n])
        merged = term if merged is None else merged + term
    out = _dot(merged.astype(BF16), wo_ref[...])
    ms = jnp.mean(out * out, axis=-1, keepdims=True)
    resid = x_ref[...]
    if x_unpadded and row0 == 0:
        resid = _with_front_rows(resid, front_ref, t == 0)
    res = resid + out * lax.rsqrt(ms + NORM_EPS) * np_ref[...]
    if row0 < PAD_FRONT:
        res = jnp.where((row0 + t * rows + _iota(res.shape, 0)) >= PAD_FRONT, res, 0.0)
    o_ref[...] = res


def _out(yssd, ydn, main, x, front, sc_w, w_branch, w_out, norm_post, batch, tp, rt, row0, x_unpadded):
    nt = (tp - row0) // rt
    rows = batch * nt * rt
    _, _, const, _ = _row_specs(rt, nt)
    if row0 == 0:
        tile, halo, _, _ = _row_specs(rt, nt)
    else:
        def tile(width, col):
            return pl.BlockSpec(
                (pl.Element(rt), pl.Element(width)),
                lambda b, t: ((b * (tp // HALO) + row0 // HALO + t * (rt // HALO)) * HALO, col * width))

        def halo(width, col):
            return pl.BlockSpec(
                (pl.Element(HALO), pl.Element(width)),
                lambda b, t: ((b * (tp // HALO) + row0 // HALO + t * (rt // HALO) - 1) * HALO, col * width))

    out_spec = pl.BlockSpec((rt, D_MODEL), lambda b, t: (b * nt + t, 0))
    if not x_unpadded:
        x_spec = tile(D_MODEL, 0)
    elif row0 == 0:
        x_spec = _unpadded_row_spec(rt, nt, tp - CHUNK)(lambda b, t: (b, t))
    else:
        x_spec = out_spec
    return pl.pallas_call(
        functools.partial(_out_kernel, row0=row0, x_unpadded=x_unpadded),
        grid=(batch, nt),
        in_specs=[
            tile(D_MODEL, 0), tile(D_MODEL, 0),
            tile(D_MODEL, COL_SC_B), tile(D_MODEL, COL_SC_C), halo(D_MODEL, COL_SC_C),
            tile(D_MODEL, COL_SC_H), halo(D_MODEL, COL_SC_H), tile(D_MODEL, COL_SC_G),
            tile(3 * D_MODEL, COL_GATES), x_spec, const((CHUNK, D_MODEL)),
            const((3, D_MODEL)), const((3, D_MODEL, D_MODEL)), const((D_MODEL, D_MODEL)),
            const((1, D_MODEL)),
        ],
        out_specs=out_spec,
        out_shape=jax.ShapeDtypeStruct((rows, D_MODEL), F32),
        compiler_params=pltpu.CompilerParams(
            dimension_semantics=("arbitrary", "arbitrary"), vmem_limit_bytes=VMEM_LIMIT),
        name="out_proj",
    )(yssd, ydn, main, main, main, main, main, main, main, x, front, sc_w, w_branch, w_out, norm_post)


def _lane_vec(rows):
    padded = [jnp.pad(v.astype(F32), (off, LANES - off - v.shape[0])) for off, v in rows]
    padded += [jnp.zeros((LANES,), F32)] * (8 - len(padded))
    return jnp.stack(padded)


def _shift_matrix(rows):
    r = jnp.arange((CONV_TAPS - 1) * rows)[:, None]
    c = jnp.arange(HALO + rows)[None, :]
    return (c == HALO - (CONV_TAPS - 1) + r % rows + r // rows).astype(BF16)


def _split_w_in(w):
    gates = w[:, IN_OFFS[11]:IN_OFFS[12]]
    qkv_dnz = w[:, IN_OFFS[7]:IN_OFFS[9]]
    sc = w[:, IN_OFFS[3]:IN_OFFS[7]]
    ssdz_xbc = w[:, IN_OFFS[0]:IN_OFFS[2]]
    return jnp.concatenate([p.astype(BF16) for p in (gates, qkv_dnz, sc, ssdz_xbc)], axis=1)


def kernel(x, meta_tokens, norm_pre, norm_post, w_in, ssd_conv_w, ssd_conv_b, ssd_dt_bias, ssd_a_log,
           ssd_d, ssd_norm, sc_conv_w, dn_conv_w, dn_dt_bias, dn_a_log, dn_norm, w_branch, w_out):
    batch, seq, d = x.shape
    assert d == D_MODEL and seq % CHUNK == 0
    tp = CHUNK + seq
    nc = tp // CHUNK
    chunks = 3 if nc % 3 == 0 else 1
    rows = batch * tp
    tm = tp // 2 if tp % 32 == 0 else tp
    rt_out = tp // 4 if tp % (4 * HALO) == 0 else tp
    rt_last = seq // 4 if seq % (4 * CHUNK) == 0 else CHUNK
    depth = w_in.shape[0]

    front = jnp.concatenate([jnp.zeros((PAD_FRONT, d), x.dtype), meta_tokens.astype(x.dtype)], axis=0)
    h = x.reshape(batch * seq, d)

    smat = _shift_matrix(chunks * CHUNK)

    for i in range(depth):
        unpadded = i == 0
        dt0 = OFF_DT // LANES * LANES
        ba0 = OFF_BETA // LANES * LANES
        main, small = _in_proj(h, front, norm_pre[i][None, :], _split_w_in(w_in[i]),
                               w_in[i, :, dt0:dt0 + LANES], w_in[i, :, ba0:ba0 + LANES], tm, tp, unpadded)

        dn_vec = _lane_vec([(LANE_A, dn_dt_bias[i]), (LANE_A, dn_a_log[i])])
        qt, rhs, kd, lmat, amat, egl = _dn_prep(
            main, small, dn_conv_w[i], dn_vec, smat, batch, nc, chunks)
        ni = batch * nc * DN_HEADS
        lt = jnp.transpose(lmat.reshape(ni, CHUNK, CHUNK), (1, 2, 0))
        if ni < LANES:
            lt = jnp.pad(lt, ((0, 0), (0, 0), (0, LANES - ni)))

        ssd_vec = _lane_vec([(LANE_DT, ssd_dt_bias[i]), (LANE_DT, ssd_a_log[i])])
        d_exp = jnp.repeat(ssd_d[i].astype(F32), SSD_HEAD_DIM)[None, :]
        yssd = _ssd(main, small, ssd_conv_w[i], ssd_conv_b[i][None, :], ssd_vec, d_exp,
                    ssd_norm[i][None, :], smat, batch, nc, chunks)

        tmat = jnp.transpose(_dn_solve(lt)[:, :, :ni], (2, 0, 1)).astype(BF16).reshape(
            batch * nc, DN_HEADS, CHUNK, CHUNK)
        ydn = _dn_scan(qt, rhs, kd, tmat, amat, egl, main, dn_norm[i][None, :], batch, nc, chunks)

        last = i == depth - 1
        h = _out(yssd, ydn, main, h, front, sc_conv_w[i], w_branch[i].astype(BF16), w_out[i].astype(BF16),
                 norm_post[i][None, :], batch, tp, rt_last if last else rt_out, CHUNK if last else 0,
                 unpadded)

    return h.reshape(batch, seq, d)
```
